```python
import math
import jax, jax.numpy as jnp
from jax import lax
import numpy as np

D_MODEL = 1024
BATCH = 8
SEQ = 4096
DEPTH = 1

PLE_DIM = 256
D_FF = 2816
EPS = 1e-6
SSM_WIDTH = 512
SSM_GROUP = 16
SSM_GROUPS = SSM_WIDTH // SSM_GROUP
SSM_STATE = 64
DT_MIN = 1e-3
DT_MAX = 1e-1
MLA_HEADS = 8
MLA_NOPE = 64
MLA_ROPE = 32
MLA_QK = MLA_NOPE + MLA_ROPE
MLA_V = 64
MLA_WIDTH = MLA_HEADS * MLA_V
Q_LORA = 384
KV_LORA = 256
ROPE_THETA = 10000.0
Q_BLOCK = 128
MIX_WIDTH = SSM_WIDTH + MLA_WIDTH
IN_WIDTH = SSM_WIDTH + Q_LORA + KV_LORA + MLA_ROPE
SPLITS = (SSM_WIDTH, SSM_WIDTH + Q_LORA, SSM_WIDTH + Q_LORA + KV_LORA)

kernel_name = 's5_mla_hymba_macaron'


def rms_norm(x, g):
    xf = x.astype(jnp.float32)
    y = xf * lax.rsqrt(jnp.mean(xf * xf, axis=-1, keepdims=True) + EPS)
    return (y * g.astype(jnp.float32)).astype(x.dtype)


def swiglu(x, w_gate, w_up, w_down):
    return (jax.nn.silu(x @ w_gate) * (x @ w_up)) @ w_down


def apply_rope(x, positions):
    half = x.shape[-1] // 2
    inv = ROPE_THETA ** (-jnp.arange(half, dtype=jnp.float32) / half)
    ang = positions.astype(jnp.float32)[..., None] * inv
    cos = jnp.cos(ang)[:, :, None, :]
    sin = jnp.sin(ang)[:, :, None, :]
    xf = x.astype(jnp.float32)
    x1, x2 = xf[..., :half], xf[..., half:]
    out = jnp.concatenate([x1 * cos - x2 * sin, x1 * sin + x2 * cos], axis=-1)
    return out.astype(x.dtype)


def s5_mixer(u, lam_re, lam_im, log_dt, b_re, b_im, c_re, c_im, d_skip, w_glu, b_glu):
    bsz, seq, _ = u.shape
    uf = u.astype(jnp.float32).reshape(bsz, seq, SSM_GROUPS, SSM_GROUP)
    dt = jnp.exp(log_dt.astype(jnp.float32))[:, None]
    lam = lax.complex(lam_re.astype(jnp.float32), lam_im.astype(jnp.float32))
    lam_bar = jnp.exp(lam * dt)
    coef = (lam_bar - 1.0) / lam
    b = lax.complex(b_re.astype(jnp.float32), b_im.astype(jnp.float32))
    b_bar = coef[..., None] * b
    bu = lax.complex(jnp.einsum('blgh,gph->blgp', uf, b_bar.real),
                     jnp.einsum('blgh,gph->blgp', uf, b_bar.imag))
    a = jnp.broadcast_to(lam_bar, (1, seq, SSM_GROUPS, SSM_STATE))

    def combine(e1, e2):
        a1, s1 = e1
        a2, s2 = e2
        return a2 * a1, a2 * s1 + s2

    _, states = lax.associative_scan(combine, (a, bu), axis=1)
    y = (jnp.einsum('blgp,ghp->blgh', states.real, c_re.astype(jnp.float32))
         - jnp.einsum('blgp,ghp->blgh', states.imag, c_im.astype(jnp.float32)))
    y = y + d_skip.astype(jnp.float32).reshape(SSM_GROUPS, SSM_GROUP) * uf
    y = jax.nn.gelu(y.reshape(bsz, seq, SSM_WIDTH))
    y = y * jax.nn.sigmoid(y @ w_glu.astype(jnp.float32) + b_glu.astype(jnp.float32))
    return y.astype(u.dtype)


def causal_attention(q, k, v):
    bsz, seq, heads, dqk = q.shape
    scale = dqk ** -0.5
    nb = seq // Q_BLOCK
    qb = q.reshape(bsz, nb, Q_BLOCK, heads, dqk).swapaxes(0, 1)
    starts = jnp.arange(nb, dtype=jnp.int32) * Q_BLOCK
    kpos = jnp.arange(seq, dtype=jnp.int32)
    qoff = jnp.arange(Q_BLOCK, dtype=jnp.int32)

    def one_block(args):
        qi, s0 = args
        s = jnp.einsum('bqhd,bkhd->bhqk', qi, k).astype(jnp.float32) * scale
        mask = (s0 + qoff)[:, None] >= kpos[None, :]
        s = jnp.where(mask[None, None], s, -jnp.inf)
        pr = jax.nn.softmax(s, axis=-1).astype(v.dtype)
        return jnp.einsum('bhqk,bkhd->bqhd', pr, v)

    o = lax.map(one_block, (qb, starts))
    return o.swapaxes(0, 1).reshape(bsz, seq, heads * v.shape[-1])


def mla_mixer(c_q, c_kv, k_pe, positions, q_norm, w_uq, kv_norm, w_ukv, qk_norm_q, qk_norm_k):
    bsz, seq, _ = c_q.shape
    q = (rms_norm(c_q, q_norm) @ w_uq).reshape(bsz, seq, MLA_HEADS, MLA_QK)
    kv = (rms_norm(c_kv, kv_norm) @ w_ukv).reshape(bsz, seq, MLA_HEADS, MLA_NOPE + MLA_V)
    k_nope, v = kv[..., :MLA_NOPE], kv[..., MLA_NOPE:]
    k_rope = jnp.broadcast_to(k_pe[:, :, None, :], (bsz, seq, MLA_HEADS, MLA_ROPE))
    k = jnp.concatenate([k_nope, k_rope], axis=-1)
    q = rms_norm(q, qk_norm_q)
    k = rms_norm(k, qk_norm_k)
    q = jnp.concatenate([q[..., :MLA_NOPE], apply_rope(q[..., MLA_NOPE:], positions)], axis=-1)
    k = jnp.concatenate([k[..., :MLA_NOPE], apply_rope(k[..., MLA_NOPE:], positions)], axis=-1)
    return causal_attention(q, k, v)


def setup_inputs(seed: int = 0) -> dict:
    key = jax.random.key(seed)
    ks = iter(jax.random.split(key, 48))

    def nrm(shape, scale):
        return jax.random.normal(next(ks), shape, jnp.float32) * scale

    def gain(dim):
        return 1.0 + nrm((DEPTH, dim), 0.02)

    L = DEPTH
    x = nrm((BATCH, SEQ, D_MODEL), 1.0)
    p = nrm((DEPTH, BATCH, SEQ, PLE_DIM), 1.0)
    positions = (jnp.arange(SEQ, dtype=jnp.int32)[None, :]
                 + jax.random.randint(next(ks), (BATCH, 1), 0, 512, dtype=jnp.int32))
    lam_re = -0.5 * (1.0 + nrm((L, SSM_GROUPS, SSM_STATE), 0.02))
    lam_im = jnp.broadcast_to(math.pi * jnp.arange(SSM_STATE, dtype=jnp.float32),
                              (L, SSM_GROUPS, SSM_STATE))
    log_dt = jax.random.uniform(next(ks), (L, SSM_GROUPS), jnp.float32,
                                math.log(DT_MIN), math.log(DT_MAX))
    return {
        'x': x, 'p': p, 'positions': positions,
        'norm_ffn1': gain(D_MODEL),
        'ffn1_w_gate': nrm((L, D_MODEL, D_FF), D_MODEL ** -0.5),
        'ffn1_w_up': nrm((L, D_MODEL, D_FF), D_MODEL ** -0.5),
        'ffn1_w_down': nrm((L, D_FF, D_MODEL), D_FF ** -0.5),
        'norm_mix': gain(D_MODEL),
        'w_in': nrm((L, D_MODEL, IN_WIDTH), D_MODEL ** -0.5),
        'ssm_lam_re': lam_re,
        'ssm_lam_im': lam_im,
        'ssm_log_dt': log_dt,
        'ssm_b_re': nrm((L, SSM_GROUPS, SSM_STATE, SSM_GROUP), (2 * SSM_GROUP) ** -0.5),
        'ssm_b_im': nrm((L, SSM_GROUPS, SSM_STATE, SSM_GROUP), (2 * SSM_GROUP) ** -0.5),
        'ssm_c_re': nrm((L, SSM_GROUPS, SSM_GROUP, SSM_STATE), (2 * SSM_STATE) ** -0.5),
        'ssm_c_im': nrm((L, SSM_GROUPS, SSM_GROUP, SSM_STATE), (2 * SSM_STATE) ** -0.5),
        'ssm_d': nrm((L, SSM_WIDTH), 1.0),
        'ssm_w_glu': nrm((L, SSM_WIDTH, SSM_WIDTH), SSM_WIDTH ** -0.5),
        'ssm_b_glu': nrm((L, SSM_WIDTH), 0.01),
        'mla_q_norm': gain(Q_LORA),
        'mla_w_uq': nrm((L, Q_LORA, MLA_HEADS * MLA_QK), Q_LORA ** -0.5),
        'mla_kv_norm': gain(KV_LORA),
        'mla_w_ukv': nrm((L, KV_LORA, MLA_HEADS * (MLA_NOPE + MLA_V)), KV_LORA ** -0.5),
        'mla_qk_norm_q': gain(MLA_QK),
        'mla_qk_norm_k': gain(MLA_QK),
        'out_norm_ssm': gain(SSM_WIDTH),
        'out_norm_attn': gain(MLA_WIDTH),
        'w_out': nrm((L, MIX_WIDTH, D_MODEL), MIX_WIDTH ** -0.5),
        'norm_ffn2': gain(D_MODEL),
        'ffn2_w_gate': nrm((L, D_MODEL, D_FF), D_MODEL ** -0.5),
        'ffn2_w_up': nrm((L, D_MODEL, D_FF), D_MODEL ** -0.5),
        'ffn2_w_down': nrm((L, D_FF, D_MODEL), D_FF ** -0.5),
        'norm_ple': gain(D_MODEL),
        'ple_w_gate': nrm((L, D_MODEL, D_MODEL), D_MODEL ** -0.5),
        'ple_w_proj': nrm((L, PLE_DIM, D_MODEL), PLE_DIM ** -0.5),
    }


def reference(x, p, positions, norm_ffn1, ffn1_w_gate, ffn1_w_up, ffn1_w_down,
              norm_mix, w_in, ssm_lam_re, ssm_lam_im, ssm_log_dt, ssm_b_re, ssm_b_im,
              ssm_c_re, ssm_c_im, ssm_d, ssm_w_glu, ssm_b_glu,
              mla_q_norm, mla_w_uq, mla_kv_norm, mla_w_ukv, mla_qk_norm_q, mla_qk_norm_k,
              out_norm_ssm, out_norm_attn, w_out,
              norm_ffn2, ffn2_w_gate, ffn2_w_up, ffn2_w_down,
              norm_ple, ple_w_gate, ple_w_proj):
    h = x
    for i in range(DEPTH):
        h = h + 0.5 * swiglu(rms_norm(h, norm_ffn1[i]), ffn1_w_gate[i], ffn1_w_up[i], ffn1_w_down[i])
        n = rms_norm(h, norm_mix[i])
        z = n @ w_in[i]
        u, c_q, c_kv, k_pe = jnp.split(z, SPLITS, axis=-1)
        y_ssm = s5_mixer(u, ssm_lam_re[i], ssm_lam_im[i], ssm_log_dt[i], ssm_b_re[i], ssm_b_im[i],
                         ssm_c_re[i], ssm_c_im[i], ssm_d[i], ssm_w_glu[i], ssm_b_glu[i])
        y_att = mla_mixer(c_q, c_kv, k_pe, positions, mla_q_norm[i], mla_w_uq[i],
                          mla_kv_norm[i], mla_w_ukv[i], mla_qk_norm_q[i], mla_qk_norm_k[i])
        y = jnp.concatenate([rms_norm(y_ssm, out_norm_ssm[i]),
                             rms_norm(y_att, out_norm_attn[i])], axis=-1)
        h = h + y @ w_out[i]
        h = h + 0.5 * swiglu(rms_norm(h, norm_ffn2[i]), ffn2_w_gate[i], ffn2_w_up[i], ffn2_w_down[i])
        gate = jax.nn.sigmoid(rms_norm(h, norm_ple[i]) @ ple_w_gate[i])
        h = h + gate * (p[i] @ ple_w_proj[i])
    return h.astype(x.dtype)
```

```python
import functools
import math

import jax
import jax.numpy as jnp
import numpy as np
from jax import lax
from jax.experimental import pallas as pl
from jax.experimental.pallas import tpu as pltpu

F32 = jnp.float32
BF16 = jnp.bfloat16

D_MODEL = 1024
BATCH = 8
SEQ = 4096
PLE_DIM = 256
D_FF = 2816
EPS = 1e-6
SSM_WIDTH = 512
SSM_GROUP = 16
SSM_GROUPS = 32
SSM_STATE = 64
MLA_HEADS = 8
MLA_NOPE = 64
MLA_ROPE = 32
MLA_QK = 96
MLA_V = 64
Q_LORA = 384
KV_LORA = 256
ROPE_THETA = 10000.0
IN_WIDTH = SSM_WIDTH + Q_LORA + KV_LORA + MLA_ROPE

LANES = 128
SUBLANES = 8
HEAD_PAD = LANES
QK_PAD = MLA_HEADS * HEAD_PAD
IN_PAD = 1280
ROW_TILE = 512
FF_CHUNKS = ((0, 1024), (1024, 1024), (2048, 768))
SSM_HALF_STATES = SSM_GROUPS * SSM_STATE // 2
SCAN_T = 32
SCAN_ROWS = SCAN_T * BATCH
ATT_TQ = 256
ATT_TK = 256
VMEM_LIMIT = 56 * 1024 * 1024


def _const_spec(shape):
    nd = len(shape)
    return pl.BlockSpec(shape, lambda *_: (0,) * nd, pipeline_mode=pl.Buffered(1))


def _rms(x, g):
    ms = jnp.mean(x * x, axis=-1, keepdims=True)
    return x * lax.rsqrt(ms + EPS) * g


def _dot(a, b):
    return jnp.dot(a, b, preferred_element_type=F32)


def _swiglu(n, wg_ref, wu_ref, wd_ref):
    acc = None
    for c0, cw in FF_CHUNKS:
        g = _dot(n, wg_ref[:, c0:c0 + cw])
        u = _dot(n, wu_ref[:, c0:c0 + cw])
        a = (g * jax.nn.sigmoid(g) * u).astype(BF16)
        d = _dot(a, wd_ref[c0:c0 + cw, :])
        acc = d if acc is None else acc + d
    return acc


def _ffn1_mix_kernel(x_ref, pos_ref, g1_ref, wg_ref, wu_ref, wd_ref, gmix_ref, win_ref,
                     gq_ref, wuq_ref, gkv_ref, wukv_ref, gqkq_ref, gqkk_ref, invf_ref,
                     h_ref, u_ref, q_ref, k_ref, v_ref):
    x = x_ref[...]
    h = x + 0.5 * _swiglu(_rms(x, g1_ref[...]).astype(BF16), wg_ref, wu_ref, wd_ref)
    h_ref[...] = h

    z = _dot(_rms(h, gmix_ref[...]).astype(BF16), win_ref[...])
    u_ref[...] = z[:, :SSM_WIDTH]
    c_q = z[:, SSM_WIDTH:SSM_WIDTH + Q_LORA]
    c_kv = z[:, SSM_WIDTH + Q_LORA:SSM_WIDTH + Q_LORA + KV_LORA]
    k_pe = z[:, SSM_WIDTH + Q_LORA + KV_LORA:]
    qf = _dot(_rms(c_q, gq_ref[...]).astype(BF16), wuq_ref[...])
    kvf = _dot(_rms(c_kv, gkv_ref[...]).astype(BF16), wukv_ref[...])
    v_ref[...] = kvf[:, QK_PAD:].astype(BF16)

    ang = pos_ref[...] * invf_ref[...]
    cosv = jnp.cos(ang)
    sinv = jnp.sin(ang)
    lane = lax.broadcasted_iota(jnp.int32, (1, LANES), 1)
    x1_lanes = (lane >= MLA_NOPE) & (lane < MLA_NOPE + MLA_ROPE // 2)
    x2_lanes = (lane >= MLA_NOPE + MLA_ROPE // 2) & (lane < MLA_QK)
    s_from_x2 = jnp.where(x1_lanes, -sinv, 0.0)
    s_from_x1 = jnp.where(x2_lanes, sinv, 0.0)
    gq = gqkq_ref[...] * (MLA_QK ** -0.5)
    gk = gqkk_ref[...]

    def norm_rope(t, g):
        t = t * lax.rsqrt(jnp.sum(t * t, axis=-1, keepdims=True) * (1.0 / MLA_QK) + EPS) * g
        half = MLA_ROPE // 2
        return (t * cosv + pltpu.roll(t, LANES - half, 1) * s_from_x2
                + pltpu.roll(t, half, 1) * s_from_x1)

    for hd in range(MLA_HEADS):
        sl = slice(hd * HEAD_PAD, (hd + 1) * HEAD_PAD)
        q_ref[:, sl] = norm_rope(qf[:, sl], gq).astype(BF16)
        k_ref[:, sl] = norm_rope(kvf[:, sl] + k_pe, gk).astype(BF16)


def _ffn1_mix(x, pos, g1, wg, wu, wd, gmix, win, gq, wuq, gkv, wukv, gqkq, gqkk, invf):
    tm = ROW_TILE
    row = lambda w: pl.BlockSpec((None, tm, w), lambda b, i: (b, i, 0))
    consts = (g1, wg, wu, wd, gmix, win, gq, wuq, gkv, wukv, gqkq, gqkk, invf)
    return pl.pallas_call(
        _ffn1_mix_kernel,
        grid=(BATCH, SEQ // tm),
        in_specs=[row(D_MODEL), row(1)] + [_const_spec(c.shape) for c in consts],
        out_specs=[row(D_MODEL), row(SSM_WIDTH), row(QK_PAD), row(QK_PAD), row(QK_PAD)],
        out_shape=[jax.ShapeDtypeStruct((BATCH, SEQ, D_MODEL), F32),
                   jax.ShapeDtypeStruct((BATCH, SEQ, SSM_WIDTH), F32),
                   jax.ShapeDtypeStruct((BATCH, SEQ, QK_PAD), BF16),
                   jax.ShapeDtypeStruct((BATCH, SEQ, QK_PAD), BF16),
                   jax.ShapeDtypeStruct((BATCH, SEQ, QK_PAD), BF16)],
        compiler_params=pltpu.CompilerParams(
            dimension_semantics=("parallel", "parallel"), vmem_limit_bytes=VMEM_LIMIT),
        name="ffn1_mix",
    )(x, pos, *consts)


def _s5_kernel(u_ref, perm_ref, permt_ref, lam_re_ref, lam_im_ref, logdt_ref,
               bre_ref, bim_ref, wcr_ref, wci_ref, d_ref, wglu_ref, bglu_ref,
               y_ref, wb_ref, lbar_ref, state_ref, bu_ref, s_ref):
    nh = SSM_HALF_STATES

    @pl.when(pl.program_id(0) == 0)
    def _prepare():
        lr = lam_re_ref[...]
        li = lam_im_ref[...]
        dt = jnp.exp(logdt_ref[...])
        mag = jnp.exp(lr * dt)
        br = mag * jnp.cos(li * dt)
        bi = mag * jnp.sin(li * dt)
        den = lr * lr + li * li
        cr = ((br - 1.0) * lr + bi * li) / den
        ci = (bi * lr - (br - 1.0) * li) / den
        for k in range(2):
            ks = slice(k * nh, (k + 1) * nh)
            lbar_ref[k, 0:1, :] = br[:, ks]
            lbar_ref[k, 1:2, :] = bi[:, ks]
            wb_ref[k, :, :nh] = (bre_ref[k] * cr[:, ks] - bim_ref[k] * ci[:, ks]).astype(BF16)
            wb_ref[k, :, nh:] = (bre_ref[k] * ci[:, ks] + bim_ref[k] * cr[:, ks]).astype(BF16)
        state_ref[...] = jnp.zeros_like(state_ref)

    u_bt = u_ref[...].reshape(SCAN_ROWS, SSM_WIDTH)
    u_tb = _dot(perm_ref[...], u_bt.astype(BF16)).astype(BF16)
    half_in = SSM_WIDTH // 2
    for k in range(2):
        bu_ref[:, k * 2 * nh:(k + 1) * 2 * nh] = _dot(u_tb[:, k * half_in:(k + 1) * half_in],
                                                      wb_ref[k])

    for k in range(2):
        c_re = k * 2 * nh
        c_im = c_re + nh
        lr = jnp.broadcast_to(lbar_ref[k, 0:1, :], (SUBLANES, nh))
        li = jnp.broadcast_to(lbar_ref[k, 1:2, :], (SUBLANES, nh))

        def advance(sr, si, r):
            nr = lr * sr - li * si + bu_ref[pl.ds(r, SUBLANES), c_re:c_re + nh]
            ni = lr * si + li * sr + bu_ref[pl.ds(r, SUBLANES), c_im:c_im + nh]
            return nr, ni

        def step(t2, carry):
            sr, si = carry
            r = pl.multiple_of(t2 * 2 * SUBLANES, 2 * SUBLANES)
            ar, ai = advance(sr, si, r)
            br_, bi_ = advance(ar, ai, r + SUBLANES)
            s_ref[pl.ds(r, 2 * SUBLANES), c_re:c_re + nh] = (
                jnp.concatenate([ar, br_], axis=0).astype(BF16))
            s_ref[pl.ds(r, 2 * SUBLANES), c_im:c_im + nh] = (
                jnp.concatenate([ai, bi_], axis=0).astype(BF16))
            return br_, bi_

        sr, si = lax.fori_loop(0, SCAN_T // 2, step, (state_ref[k, 0], state_ref[k, 1]))
        state_ref[k, 0] = sr
        state_ref[k, 1] = si

    ys = []
    for k in range(2):
        c_re = k * 2 * nh
        c_im = c_re + nh
        ys.append(_dot(s_ref[:, c_re:c_re + nh], wcr_ref[k])
                  - _dot(s_ref[:, c_im:c_im + nh], wci_ref[k]))
    y_tb = jnp.concatenate(ys, axis=1).astype(BF16)
    y = _dot(permt_ref[...], y_tb) + d_ref[...] * u_bt
    y = jax.nn.gelu(y)
    y = y * jax.nn.sigmoid(_dot(y.astype(BF16), wglu_ref[...]) + bglu_ref[...])
    y_ref[...] = y.reshape(BATCH, SCAN_T, SSM_WIDTH)


def _s5(u, perm, permt, lam_re, lam_im, logdt, bre, bim, wcr, wci, d, wglu, bglu):
    nh = SSM_HALF_STATES
    consts = (perm, permt, lam_re, lam_im, logdt, bre, bim, wcr, wci, d, wglu, bglu)
    blk = pl.BlockSpec((BATCH, SCAN_T, SSM_WIDTH), lambda c: (0, c, 0))
    return pl.pallas_call(
        _s5_kernel,
        grid=(SEQ // SCAN_T,),
        in_specs=[blk] + [_const_spec(c.shape) for c in consts],
        out_specs=blk,
        out_shape=jax.ShapeDtypeStruct((BATCH, SEQ, SSM_WIDTH), F32),
        scratch_shapes=[
            pltpu.VMEM((2, SSM_WIDTH // 2, 2 * nh), BF16),
            pltpu.VMEM((2, 2, nh), F32),
            pltpu.VMEM((2, 2, SUBLANES, nh), F32),
            pltpu.VMEM((SCAN_ROWS, 4 * nh), F32),
            pltpu.VMEM((SCAN_ROWS, 4 * nh), BF16),
        ],
        compiler_params=pltpu.CompilerParams(
            dimension_semantics=("arbitrary",), vmem_limit_bytes=VMEM_LIMIT),
        name="s5",
    )(u, *consts)


def _attn_kernel(q_ref, k_ref, v_ref, o_ref):
    qi = pl.program_id(1)
    tq, tk = ATT_TQ, ATT_TK
    nt = (((1,), (1,)), ((), ()))
    row = lax.broadcasted_iota(jnp.int32, (tq, tk), 0)
    col = lax.broadcasted_iota(jnp.int32, (tq, tk), 1)
    diag_mask = row >= col
    outs = []
    for hd in range(MLA_HEADS):
        sl = slice(hd * HEAD_PAD, (hd + 1) * HEAD_PAD)
        qh = q_ref[:, sl]

        def tile(j, carry, masked):
            m, l, acc = carry
            r0 = pl.multiple_of(j * tk, tk)
            s = lax.dot_general(qh, k_ref[pl.ds(r0, tk), sl], nt, preferred_element_type=F32)
            if masked:
                s = jnp.where(diag_mask, s, -jnp.inf)
            m_new = jnp.maximum(m, jnp.max(s, axis=-1, keepdims=True))
            alpha = jnp.exp(m - m_new)
            p = jnp.exp(s - m_new)
            l = alpha * l + jnp.sum(p, axis=-1, keepdims=True)
            acc = alpha * acc + _dot(p.astype(BF16), v_ref[pl.ds(r0, tk), sl])
            return m_new, l, acc

        init = (jnp.full((tq, 1), -jnp.inf, F32), jnp.zeros((tq, 1), F32),
                jnp.zeros((tq, HEAD_PAD), F32))
        carry = lax.fori_loop(0, qi, functools.partial(tile, masked=False), init)
        m, l, acc = tile(qi, carry, True)
        outs.append((acc / l)[:, :MLA_V])
    o_ref[...] = jnp.concatenate(outs, axis=1)


def _attn(q, k, v):
    kv_spec = pl.BlockSpec((None, SEQ, QK_PAD), lambda b, i: (b, 0, 0))
    return pl.pallas_call(
        _attn_kernel,
        grid=(BATCH, SEQ // ATT_TQ),
        in_specs=[pl.BlockSpec((None, ATT_TQ, QK_PAD), lambda b, i: (b, i, 0)), kv_spec, kv_spec],
        out_specs=pl.BlockSpec((None, ATT_TQ, MLA_HEADS * MLA_V), lambda b, i: (b, i, 0)),
        out_shape=jax.ShapeDtypeStruct((BATCH, SEQ, MLA_HEADS * MLA_V), F32),
        compiler_params=pltpu.CompilerParams(
            dimension_semantics=("parallel", "arbitrary"), vmem_limit_bytes=VMEM_LIMIT),
        name="attn",
    )(q, k, v)


def _out_ffn2_kernel(h_ref, ys_ref, ya_ref, p_ref, gs_ref, ga_ref, wos_ref, woa_ref,
                     g2_ref, wg_ref, wu_ref, wd_ref, gple_ref, wgate_ref, wproj_ref, o_ref):
    h = (h_ref[...]
         + _dot(_rms(ys_ref[...], gs_ref[...]).astype(BF16), wos_ref[...])
         + _dot(_rms(ya_ref[...], ga_ref[...]).astype(BF16), woa_ref[...]))
    h = h + 0.5 * _swiglu(_rms(h, g2_ref[...]).astype(BF16), wg_ref, wu_ref, wd_ref)
    gate = jax.nn.sigmoid(_dot(_rms(h, gple_ref[...]).astype(BF16), wgate_ref[...]))
    o_ref[...] = h + gate * _dot(p_ref[...].astype(BF16), wproj_ref[...])


def _out_ffn2(h, ys, ya, p, gs, ga, wos, woa, g2, wg, wu, wd, gple, wgate, wproj):
    tm = ROW_TILE
    row = lambda w: pl.BlockSpec((None, tm, w), lambda b, i: (b, i, 0))
    consts = (gs, ga, wos, woa, g2, wg, wu, wd, gple, wgate, wproj)
    return pl.pallas_call(
        _out_ffn2_kernel,
        grid=(BATCH, SEQ // tm),
        in_specs=[row(D_MODEL), row(SSM_WIDTH), row(SSM_WIDTH), row(PLE_DIM)]
                 + [_const_spec(c.shape) for c in consts],
        out_specs=row(D_MODEL),
        out_shape=jax.ShapeDtypeStruct((BATCH, SEQ, D_MODEL), F32),
        compiler_params=pltpu.CompilerParams(
            dimension_semantics=("parallel", "parallel"), vmem_limit_bytes=VMEM_LIMIT),
        name="out_ffn2",
    )(h, ys, ya, p, *consts)


def _pad_heads(w, width):
    w = w.reshape(w.shape[0], MLA_HEADS, width)
    return jnp.pad(w, ((0, 0), (0, 0), (0, HEAD_PAD - width))).reshape(w.shape[0], QK_PAD)


def _block_diag(blocks):
    g, r, c = blocks.shape
    eye = jnp.eye(g, dtype=blocks.dtype)
    return (blocks[:, :, None, :] * eye[:, None, :, None]).reshape(g * r, g * c)


def _permutation():
    rows = np.arange(SCAN_ROWS)
    t, b = rows // BATCH, rows % BATCH
    perm = np.zeros((SCAN_ROWS, SCAN_ROWS), np.float32)
    perm[rows, b * SCAN_T + t] = 1.0
    return perm


def kernel(x, p, positions, norm_ffn1, ffn1_w_gate, ffn1_w_up, ffn1_w_down, norm_mix, w_in,
           ssm_lam_re, ssm_lam_im, ssm_log_dt, ssm_b_re, ssm_b_im, ssm_c_re, ssm_c_im, ssm_d,
           ssm_w_glu, ssm_b_glu, mla_q_norm, mla_w_uq, mla_kv_norm, mla_w_ukv, mla_qk_norm_q,
           mla_qk_norm_k, out_norm_ssm, out_norm_attn, w_out, norm_ffn2, ffn2_w_gate, ffn2_w_up,
           ffn2_w_down, norm_ple, ple_w_gate, ple_w_proj):
    h = x
    for i in range(norm_ffn1.shape[0]):
        vec = lambda a: a[i].reshape(1, -1).astype(F32)
        bf = lambda a: a.astype(BF16)

        pos = positions.astype(F32)[..., None]
        win = jnp.zeros((D_MODEL, IN_PAD), F32)
        win = win.at[:, :IN_WIDTH - MLA_ROPE].set(w_in[i][:, :IN_WIDTH - MLA_ROPE])
        kpe0 = IN_WIDTH - MLA_ROPE + MLA_NOPE
        win = win.at[:, kpe0:kpe0 + MLA_ROPE].set(w_in[i][:, IN_WIDTH - MLA_ROPE:])
        wuq = _pad_heads(mla_w_uq[i], MLA_QK)
        wukv = mla_w_ukv[i].reshape(KV_LORA, MLA_HEADS, MLA_NOPE + MLA_V)
        wukv = jnp.concatenate([_pad_heads(wukv[..., :MLA_NOPE].reshape(KV_LORA, -1), MLA_NOPE),
                                _pad_heads(wukv[..., MLA_NOPE:].reshape(KV_LORA, -1), MLA_V)], 1)
        pad_gain = lambda g: jnp.pad(g[i].astype(F32), (0, HEAD_PAD - MLA_QK)).reshape(1, HEAD_PAD)
        half = MLA_ROPE // 2
        inv = ROPE_THETA ** (-jnp.arange(half, dtype=F32) / half)
        invf = jnp.zeros((1, HEAD_PAD), F32)
        invf = invf.at[0, MLA_NOPE:MLA_NOPE + half].set(inv).at[0, MLA_NOPE + half:MLA_QK].set(inv)

        h1, u, q, k, v = _ffn1_mix(
            h, pos, vec(norm_ffn1), bf(ffn1_w_gate[i]), bf(ffn1_w_up[i]), bf(ffn1_w_down[i]),
            vec(norm_mix), bf(win), vec(mla_q_norm), bf(wuq), vec(mla_kv_norm), bf(wukv),
            pad_gain(mla_qk_norm_q), pad_gain(mla_qk_norm_k), invf)

        hg = SSM_GROUPS // 2
        cols = lambda a: a[i].astype(F32).reshape(1, SSM_GROUPS * SSM_STATE)
        logdt = jnp.repeat(ssm_log_dt[i].astype(F32), SSM_STATE).reshape(1, -1)
        b_t = lambda b: jnp.stack([_block_diag(jnp.swapaxes(b[i][kk * hg:(kk + 1) * hg], 1, 2))
                                   for kk in range(2)]).astype(F32)
        c_t = lambda c: jnp.stack([_block_diag(jnp.swapaxes(c[i][kk * hg:(kk + 1) * hg], 1, 2))
                                   for kk in range(2)]).astype(BF16)
        perm = _permutation()
        y_ssm = _s5(u, jnp.asarray(perm, BF16), jnp.asarray(perm.T, BF16),
                    cols(ssm_lam_re), cols(ssm_lam_im), logdt, b_t(ssm_b_re), b_t(ssm_b_im),
                    c_t(ssm_c_re), c_t(ssm_c_im), vec(ssm_d), bf(ssm_w_glu[i]), vec(ssm_b_glu))

        y_att = _attn(q, k, v)

        h = _out_ffn2(h1, y_ssm, y_att, p[i], vec(out_norm_ssm), vec(out_norm_attn),
                      bf(w_out[i][:SSM_WIDTH]), bf(w_out[i][SSM_WIDTH:]), vec(norm_ffn2),
                      bf(ffn2_w_gate[i]), bf(ffn2_w_up[i]), bf(ffn2_w_down[i]), vec(norm_ple),
                      bf(ple_w_gate[i]), bf(ple_w_proj[i]))
    return h.astype(x.dtype)
```

```python
import functools
import math

import jax
import jax.numpy as jnp
import numpy as np
from jax import lax
from jax.experimental import pallas as pl
from jax.experimental.pallas import tpu as pltpu

F32 = jnp.float32
BF16 = jnp.bfloat16

D_MODEL = 1024
BATCH = 8
SEQ = 4096
PLE_DIM = 256
D_FF = 2816
EPS = 1e-6
SSM_WIDTH = 512
SSM_GROUP = 16
SSM_GROUPS = 32
SSM_STATE = 64
MLA_HEADS = 8
MLA_NOPE = 64
MLA_ROPE = 32
MLA_QK = 96
MLA_V = 64
Q_LORA = 384
KV_LORA = 256
ROPE_THETA = 10000.0
IN_WIDTH = SSM_WIDTH + Q_LORA + KV_LORA + MLA_ROPE

LANES = 128
SUBLANES = 8
HEAD_PAD = LANES
QK_PAD = MLA_HEADS * HEAD_PAD
IN_PAD = 1280
ROW_TILE = 512
ROW_SPLIT = 1
SUB_ROWS = ROW_TILE // ROW_SPLIT
ATT_HG = 8
ATT_EXTRA = 16
FF_CHUNKS = ((0, 1024), (1024, 1024), (2048, 768))
SSM_HALF_STATES = SSM_GROUPS * SSM_STATE // 2
SCAN_T = 32
SCAN_ROWS = SCAN_T * BATCH
ATT_TQ = 256
ATT_TK = 256
VMEM_LIMIT = 56 * 1024 * 1024


def _const_spec(shape):
    nd = len(shape)
    return pl.BlockSpec(shape, lambda *_: (0,) * nd, pipeline_mode=pl.Buffered(1))


def _rms(x, g):
    ms = jnp.mean(x * x, axis=-1, keepdims=True)
    return x * lax.rsqrt(ms + EPS) * g


def _dot(a, b):
    return jnp.dot(a, b, preferred_element_type=F32)


def _swiglu(n, wg_ref, wu_ref, wd_ref):
    acc = None
    for c0, cw in FF_CHUNKS:
        g = _dot(n, wg_ref[:, c0:c0 + cw])
        u = _dot(n, wu_ref[:, c0:c0 + cw])
        a = (g * jax.nn.sigmoid(g) * u).astype(BF16)
        d = _dot(a, wd_ref[c0:c0 + cw, :])
        acc = d if acc is None else acc + d
    return acc


def _ffn1_mix_kernel(x_ref, posc_ref, posr_ref, g1_ref, wg_ref, wu_ref, wd_ref, gmix_ref, win_ref,
                     gq_ref, wuqt_ref, gkv_ref, wuk_ref, wvt_ref, gqkq_ref, gqkk_ref,
                     invf_ref, invc_ref, ones2_ref, rot2_ref,
                     h_ref, u_ref, qt_ref, k_ref, vt_ref):
    nt = (((1,), (1,)), ((), ()))
    half = MLA_ROPE // 2
    x = x_ref[...]
    h = x + 0.5 * _swiglu(_rms(x, g1_ref[...]).astype(BF16), wg_ref, wu_ref, wd_ref)
    h_ref[...] = h

    z = _dot(_rms(h, gmix_ref[...]).astype(BF16), win_ref[...])
    u_ref[...] = z[:, :SSM_WIDTH]
    c_q = _rms(z[:, SSM_WIDTH:SSM_WIDTH + Q_LORA], gq_ref[...]).astype(BF16)
    c_kv = _rms(z[:, SSM_WIDTH + Q_LORA:SSM_WIDTH + Q_LORA + KV_LORA], gkv_ref[...]).astype(BF16)
    k_pe = z[:, SSM_WIDTH + Q_LORA + KV_LORA:]

    vt = lax.dot_general(wvt_ref[...], c_kv, nt, preferred_element_type=F32)
    for c in range(ROW_TILE // ATT_TK):
        vt_ref[c] = vt[:, c * ATT_TK:(c + 1) * ATT_TK].astype(BF16)

    qt = lax.dot_general(wuqt_ref[...], c_q, nt, preferred_element_type=F32)
    ang_t = invc_ref[...] * posr_ref[...]
    cos_t = jnp.cos(ang_t)
    sin_t = jnp.sin(ang_t)
    gq_t = jnp.broadcast_to(gqkq_ref[...] * (MLA_QK ** -0.5 * math.log2(math.e)),
                            (HEAD_PAD, ROW_TILE))
    for hd in range(MLA_HEADS):
        t = qt[hd * HEAD_PAD:(hd + 1) * HEAD_PAD, :]
        ss = jnp.sum(t * t, axis=0, keepdims=True)
        t = t * lax.rsqrt(ss * (1.0 / MLA_QK) + EPS) * gq_t
        x1 = t[MLA_NOPE:MLA_NOPE + half, :]
        x2 = t[MLA_NOPE + half:MLA_QK, :]
        r0 = hd * HEAD_PAD
        qt_ref[r0:r0 + MLA_NOPE, :] = t[:MLA_NOPE, :].astype(BF16)
        qt_ref[r0 + MLA_NOPE:r0 + MLA_NOPE + half, :] = (x1 * cos_t - x2 * sin_t).astype(BF16)
        qt_ref[r0 + MLA_NOPE + half:r0 + MLA_QK, :] = (x1 * sin_t + x2 * cos_t).astype(BF16)
        qt_ref[r0 + MLA_QK:r0 + HEAD_PAD, :] = t[MLA_QK:, :].astype(BF16)

    kn = _dot(c_kv, wuk_ref[...])
    ang = posc_ref[...] * invf_ref[...]
    cos2 = jnp.concatenate([jnp.cos(ang)] * 2, axis=1)
    sin2 = jnp.concatenate([jnp.sin(ang)] * 2, axis=1)
    k_pe2 = jnp.concatenate([k_pe, k_pe], axis=1)
    gk2 = jnp.concatenate([gqkk_ref[...]] * 2, axis=1)
    for g in range(QK_PAD // (2 * HEAD_PAD)):
        sl = slice(g * 2 * HEAD_PAD, (g + 1) * 2 * HEAD_PAD)
        t = kn[:, sl] + k_pe2
        ss = _dot((t * t).astype(BF16), ones2_ref[...])
        t = t * lax.rsqrt(ss * (1.0 / MLA_QK) + EPS) * gk2
        k_ref[:, sl] = (t * cos2 + _dot(t.astype(BF16), rot2_ref[...]) * sin2).astype(BF16)


def _ffn1_mix(x, posc, posr, *consts):
    tm = ROW_TILE
    row = lambda w: pl.BlockSpec((None, tm, w), lambda b, i: (b, i, 0))
    return pl.pallas_call(
        _ffn1_mix_kernel,
        grid=(BATCH, SEQ // tm),
        in_specs=[row(D_MODEL), row(1), pl.BlockSpec((None, 1, tm), lambda b, i: (b, 0, i))]
                 + [_const_spec(c.shape) for c in consts],
        out_specs=[row(D_MODEL), row(SSM_WIDTH),
                   pl.BlockSpec((None, QK_PAD, tm), lambda b, i: (b, 0, i)),
                   row(QK_PAD),
                   pl.BlockSpec((None, tm // ATT_TK, MLA_HEADS * MLA_V, ATT_TK),
                                lambda b, i: (b, i, 0, 0))],
        out_shape=[jax.ShapeDtypeStruct((BATCH, SEQ, D_MODEL), F32),
                   jax.ShapeDtypeStruct((BATCH, SEQ, SSM_WIDTH), F32),
                   jax.ShapeDtypeStruct((BATCH, QK_PAD, SEQ), BF16),
                   jax.ShapeDtypeStruct((BATCH, SEQ, QK_PAD), BF16),
                   jax.ShapeDtypeStruct((BATCH, SEQ // ATT_TK, MLA_HEADS * MLA_V, ATT_TK),
                                        BF16)],
        compiler_params=pltpu.CompilerParams(
            dimension_semantics=("parallel", "parallel"), vmem_limit_bytes=VMEM_LIMIT),
        name="ffn1_mix",
    )(x, posc, posr, *consts)


def _s5_kernel(u_ref, perm_ref, permt_ref, lam_re_ref, lam_im_ref, logdt_ref,
               bre_ref, bim_ref, wcr_ref, wci_ref, d_ref, wglu_ref, bglu_ref,
               y_ref, wb_ref, lbar_ref, state_ref, bu_ref, s_ref):
    nh = SSM_HALF_STATES

    @pl.when(pl.program_id(0) == 0)
    def _prepare():
        lr = lam_re_ref[...]
        li = lam_im_ref[...]
        dt = jnp.exp(logdt_ref[...])
        mag = jnp.exp(lr * dt)
        br = mag * jnp.cos(li * dt)
        bi = mag * jnp.sin(li * dt)
        den = lr * lr + li * li
        cr = ((br - 1.0) * lr + bi * li) / den
        ci = (bi * lr - (br - 1.0) * li) / den
        for k in range(2):
            ks = slice(k * nh, (k + 1) * nh)
            lbar_ref[k, 0:1, :] = br[:, ks]
            lbar_ref[k, 1:2, :] = bi[:, ks]
            wb_ref[k, :, :nh] = (bre_ref[k] * cr[:, ks] - bim_ref[k] * ci[:, ks]).astype(BF16)
            wb_ref[k, :, nh:] = (bre_ref[k] * ci[:, ks] + bim_ref[k] * cr[:, ks]).astype(BF16)
        state_ref[...] = jnp.zeros_like(state_ref)

    u_bt = u_ref[...].reshape(SCAN_ROWS, SSM_WIDTH)
    u_tb = _dot(perm_ref[...], u_bt.astype(BF16)).astype(BF16)
    half_in = SSM_WIDTH // 2
    for k in range(2):
        bu_ref[:, k * 2 * nh:(k + 1) * 2 * nh] = _dot(u_tb[:, k * half_in:(k + 1) * half_in],
                                                      wb_ref[k])

    for k in range(2):
        c_re = k * 2 * nh
        c_im = c_re + nh
        lr = jnp.broadcast_to(lbar_ref[k, 0:1, :], (SUBLANES, nh))
        li = jnp.broadcast_to(lbar_ref[k, 1:2, :], (SUBLANES, nh))

        def advance(sr, si, r):
            nr = lr * sr - li * si + bu_ref[pl.ds(r, SUBLANES), c_re:c_re + nh]
            ni = lr * si + li * sr + bu_ref[pl.ds(r, SUBLANES), c_im:c_im + nh]
            return nr, ni

        def step(t2, carry):
            sr, si = carry
            r = pl.multiple_of(t2 * 2 * SUBLANES, 2 * SUBLANES)
            ar, ai = advance(sr, si, r)
            br_, bi_ = advance(ar, ai, r + SUBLANES)
            s_ref[pl.ds(r, 2 * SUBLANES), c_re:c_re + nh] = (
                jnp.concatenate([ar, br_], axis=0).astype(BF16))
            s_ref[pl.ds(r, 2 * SUBLANES), c_im:c_im + nh] = (
                jnp.concatenate([ai, bi_], axis=0).astype(BF16))
            return br_, bi_

        sr, si = lax.fori_loop(0, SCAN_T // 2, step, (state_ref[k, 0], state_ref[k, 1]))
        state_ref[k, 0] = sr
        state_ref[k, 1] = si

    ys = []
    for k in range(2):
        c_re = k * 2 * nh
        c_im = c_re + nh
        ys.append(_dot(s_ref[:, c_re:c_re + nh], wcr_ref[k])
                  - _dot(s_ref[:, c_im:c_im + nh], wci_ref[k]))
    y_tb = jnp.concatenate(ys, axis=1).astype(BF16)
    y = _dot(permt_ref[...], y_tb) + d_ref[...] * u_bt
    y = jax.nn.gelu(y)
    y = y * jax.nn.sigmoid(_dot(y.astype(BF16), wglu_ref[...]) + bglu_ref[...])
    y_ref[...] = y.reshape(BATCH, SCAN_T, SSM_WIDTH)


def _s5(u, perm, permt, lam_re, lam_im, logdt, bre, bim, wcr, wci, d, wglu, bglu):
    nh = SSM_HALF_STATES
    consts = (perm, permt, lam_re, lam_im, logdt, bre, bim, wcr, wci, d, wglu, bglu)
    blk = pl.BlockSpec((BATCH, SCAN_T, SSM_WIDTH), lambda c: (0, c, 0))
    return pl.pallas_call(
        _s5_kernel,
        grid=(SEQ // SCAN_T,),
        in_specs=[blk] + [_const_spec(c.shape) for c in consts],
        out_specs=blk,
        out_shape=jax.ShapeDtypeStruct((BATCH, SEQ, SSM_WIDTH), F32),
        scratch_shapes=[
            pltpu.VMEM((2, SSM_WIDTH // 2, 2 * nh), BF16),
            pltpu.VMEM((2, 2, nh), F32),
            pltpu.VMEM((2, 2, SUBLANES, nh), F32),
            pltpu.VMEM((SCAN_ROWS, 4 * nh), F32),
            pltpu.VMEM((SCAN_ROWS, 4 * nh), BF16),
        ],
        compiler_params=pltpu.CompilerParams(
            dimension_semantics=("arbitrary",), vmem_limit_bytes=VMEM_LIMIT),
        name="s5",
    )(u, *consts)


def _attn_kernel(qt_ref, k_ref, vt_ref, o_ref, m_ref, acc_ref, sta_ref, stb_ref):
    qi = pl.program_id(2)
    tq, tk = ATT_TQ, ATT_TK
    kpos = lax.broadcasted_iota(jnp.int32, (tk, tq), 0)
    qpos = lax.broadcasted_iota(jnp.int32, (tk, tq), 1)
    causal = kpos <= qpos
    ones_rows = (lax.broadcasted_iota(jnp.int32, (ATT_EXTRA, tk), 0) == 0).astype(BF16)
    m_ref[...] = jnp.full(m_ref.shape, -jnp.inf, F32)
    acc_ref[...] = jnp.zeros(acc_ref.shape, F32)

    def scores(j, st_ref):
        r0 = pl.multiple_of(j * tk, tk)
        for hd in range(ATT_HG):
            st_ref[hd] = _dot(k_ref[pl.ds(r0, tk), hd * HEAD_PAD:(hd + 1) * HEAD_PAD],
                              qt_ref[hd * HEAD_PAD:(hd + 1) * HEAD_PAD, :])

    def consume(j, st_ref, masked):
        for hd in range(ATT_HG):
            st = st_ref[hd]
            if masked:
                st = jnp.where(causal, st, -jnp.inf)
            m_old = m_ref[hd]
            m_new = jnp.maximum(m_old, jnp.max(st, axis=0, keepdims=True))
            alpha = jnp.exp2(m_old - m_new)
            pt = jnp.exp2(st - m_new).astype(BF16)
            vt = jnp.concatenate([vt_ref[j, hd * MLA_V:(hd + 1) * MLA_V, :], ones_rows], axis=0)
            acc_ref[hd] = alpha * acc_ref[hd] + _dot(vt, pt)
            m_ref[hd] = m_new

    def tile_pair(jj, carry):
        j = 2 * jj
        scores(j + 1, stb_ref)
        consume(j, sta_ref, False)
        scores(j + 2, sta_ref)
        consume(j + 1, stb_ref, False)
        return carry

    scores(0, sta_ref)
    lax.fori_loop(0, qi // 2, tile_pair, 0)

    @pl.when(qi % 2 == 0)
    def _diagonal_in_a():
        consume(qi, sta_ref, True)

    @pl.when(qi % 2 == 1)
    def _diagonal_in_b():
        scores(qi, stb_ref)
        consume(qi - 1, sta_ref, False)
        consume(qi, stb_ref, True)
    outs = [acc_ref[hd, :MLA_V, :] / acc_ref[hd, MLA_V:MLA_V + 1, :] for hd in range(ATT_HG)]
    for c in range(ATT_HG * MLA_V // ATT_TQ):
        o_ref[:, c * ATT_TQ:(c + 1) * ATT_TQ] = jnp.concatenate(
            outs[c * (ATT_TQ // MLA_V):(c + 1) * (ATT_TQ // MLA_V)], axis=0).T


def _attn(qt, k, vt):
    hg = ATT_HG
    return pl.pallas_call(
        _attn_kernel,
        grid=(BATCH, MLA_HEADS // hg, SEQ // ATT_TQ),
        in_specs=[pl.BlockSpec((None, hg * HEAD_PAD, ATT_TQ), lambda b, g, i: (b, g, i)),
                  pl.BlockSpec((None, SEQ, hg * HEAD_PAD), lambda b, g, i: (b, 0, g)),
                  pl.BlockSpec((None, SEQ // ATT_TK, hg * MLA_V, ATT_TK),
                               lambda b, g, i: (b, 0, g, 0))],
        out_specs=pl.BlockSpec((None, ATT_TQ, hg * MLA_V), lambda b, g, i: (b, i, g)),
        out_shape=jax.ShapeDtypeStruct((BATCH, SEQ, MLA_HEADS * MLA_V), F32),
        scratch_shapes=[pltpu.VMEM((hg, 1, ATT_TQ), F32),
                        pltpu.VMEM((hg, MLA_V + ATT_EXTRA, ATT_TQ), F32),
                        pltpu.VMEM((hg, ATT_TK, ATT_TQ), F32),
                        pltpu.VMEM((hg, ATT_TK, ATT_TQ), F32)],
        compiler_params=pltpu.CompilerParams(
            dimension_semantics=("parallel", "parallel", "arbitrary"),
            vmem_limit_bytes=VMEM_LIMIT),
        name="attn",
    )(qt, k, vt)


def _out_ffn2_kernel(h_ref, ys_ref, ya_ref, p_ref, gs_ref, ga_ref, wos_ref, woa_ref,
                     g2_ref, wg_ref, wu_ref, wd_ref, gple_ref, wgate_ref, wproj_ref, o_ref):
    for r in range(ROW_SPLIT):
        rows = slice(r * SUB_ROWS, (r + 1) * SUB_ROWS)
        h = (h_ref[rows, :]
             + _dot(_rms(ys_ref[rows, :], gs_ref[...]).astype(BF16), wos_ref[...])
             + _dot(_rms(ya_ref[rows, :], ga_ref[...]).astype(BF16), woa_ref[...]))
        h = h + 0.5 * _swiglu(_rms(h, g2_ref[...]).astype(BF16), wg_ref, wu_ref, wd_ref)
        gate = jax.nn.sigmoid(_dot(_rms(h, gple_ref[...]).astype(BF16), wgate_ref[...]))
        o_ref[rows, :] = h + gate * _dot(p_ref[rows, :].astype(BF16), wproj_ref[...])


def _out_ffn2(h, ys, ya, p, gs, ga, wos, woa, g2, wg, wu, wd, gple, wgate, wproj):
    tm = ROW_TILE
    row = lambda w: pl.BlockSpec((None, tm, w), lambda b, i: (b, i, 0))
    consts = (gs, ga, wos, woa, g2, wg, wu, wd, gple, wgate, wproj)
    return pl.pallas_call(
        _out_ffn2_kernel,
        grid=(BATCH, SEQ // tm),
        in_specs=[row(D_MODEL), row(SSM_WIDTH), row(SSM_WIDTH), row(PLE_DIM)]
                 + [_const_spec(c.shape) for c in consts],
        out_specs=row(D_MODEL),
        out_shape=jax.ShapeDtypeStruct((BATCH, SEQ, D_MODEL), F32),
        compiler_params=pltpu.CompilerParams(
            dimension_semantics=("parallel", "parallel"), vmem_limit_bytes=VMEM_LIMIT),
        name="out_ffn2",
    )(h, ys, ya, p, *consts)


def _pad_heads(w, width):
    w = w.reshape(w.shape[0], MLA_HEADS, width)
    return jnp.pad(w, ((0, 0), (0, 0), (0, HEAD_PAD - width))).reshape(w.shape[0], QK_PAD)


def _block_diag(blocks):
    g, r, c = blocks.shape
    eye = jnp.eye(g, dtype=blocks.dtype)
    return (blocks[:, :, None, :] * eye[:, None, :, None]).reshape(g * r, g * c)


def _head_pair_constants():
    ones2 = np.kron(np.eye(2, dtype=np.float32), np.ones((HEAD_PAD, HEAD_PAD), np.float32))
    rot = np.zeros((HEAD_PAD, HEAD_PAD), np.float32)
    half = MLA_ROPE // 2
    j = np.arange(half)
    rot[MLA_NOPE + half + j, MLA_NOPE + j] = -1.0
    rot[MLA_NOPE + j, MLA_NOPE + half + j] = 1.0
    return ones2, np.kron(np.eye(2, dtype=np.float32), rot)


def _permutation():
    rows = np.arange(SCAN_ROWS)
    t, b = rows // BATCH, rows % BATCH
    perm = np.zeros((SCAN_ROWS, SCAN_ROWS), np.float32)
    perm[rows, b * SCAN_T + t] = 1.0
    return perm


def kernel(x, p, positions, norm_ffn1, ffn1_w_gate, ffn1_w_up, ffn1_w_down, norm_mix, w_in,
           ssm_lam_re, ssm_lam_im, ssm_log_dt, ssm_b_re, ssm_b_im, ssm_c_re, ssm_c_im, ssm_d,
           ssm_w_glu, ssm_b_glu, mla_q_norm, mla_w_uq, mla_kv_norm, mla_w_ukv, mla_qk_norm_q,
           mla_qk_norm_k, out_norm_ssm, out_norm_attn, w_out, norm_ffn2, ffn2_w_gate, ffn2_w_up,
           ffn2_w_down, norm_ple, ple_w_gate, ple_w_proj):
    h = x
    for i in range(norm_ffn1.shape[0]):
        vec = lambda a: a[i].reshape(1, -1).astype(F32)
        bf = lambda a: a.astype(BF16)

        pos = positions.astype(F32)
        win = jnp.zeros((D_MODEL, IN_PAD), F32)
        win = win.at[:, :IN_WIDTH - MLA_ROPE].set(w_in[i][:, :IN_WIDTH - MLA_ROPE])
        kpe0 = IN_WIDTH - MLA_ROPE + MLA_NOPE
        win = win.at[:, kpe0:kpe0 + MLA_ROPE].set(w_in[i][:, IN_WIDTH - MLA_ROPE:])
        wuqt = _pad_heads(mla_w_uq[i], MLA_QK).T
        wukv = mla_w_ukv[i].reshape(KV_LORA, MLA_HEADS, MLA_NOPE + MLA_V)
        wuk = _pad_heads(wukv[..., :MLA_NOPE].reshape(KV_LORA, -1), MLA_NOPE)
        wvt = wukv[..., MLA_NOPE:].reshape(KV_LORA, -1).T
        pad_gain = lambda g: jnp.pad(g[i].astype(F32), (0, HEAD_PAD - MLA_QK))
        half = MLA_ROPE // 2
        inv = ROPE_THETA ** (-jnp.arange(half, dtype=F32) / half)
        invf = jnp.zeros((1, HEAD_PAD), F32)
        invf = invf.at[0, MLA_NOPE:MLA_NOPE + half].set(inv).at[0, MLA_NOPE + half:MLA_QK].set(inv)
        ones2, rot2 = _head_pair_constants()

        h1, u, qt, k, vt = _ffn1_mix(
            h, pos[..., None], pos[:, None, :],
            vec(norm_ffn1), bf(ffn1_w_gate[i]), bf(ffn1_w_up[i]), bf(ffn1_w_down[i]),
            vec(norm_mix), bf(win), vec(mla_q_norm), bf(wuqt), vec(mla_kv_norm), bf(wuk), bf(wvt),
            pad_gain(mla_qk_norm_q).reshape(HEAD_PAD, 1), pad_gain(mla_qk_norm_k).reshape(1, HEAD_PAD),
            invf, inv.reshape(half, 1), jnp.asarray(ones2, BF16), jnp.asarray(rot2, BF16))

        hg = SSM_GROUPS // 2
        cols = lambda a: a[i].astype(F32).reshape(1, SSM_GROUPS * SSM_STATE)
        logdt = jnp.repeat(ssm_log_dt[i].astype(F32), SSM_STATE).reshape(1, -1)
        b_t = lambda b: jnp.stack([_block_diag(jnp.swapaxes(b[i][kk * hg:(kk + 1) * hg], 1, 2))
                                   for kk in range(2)]).astype(F32)
        c_t = lambda c: jnp.stack([_block_diag(jnp.swapaxes(c[i][kk * hg:(kk + 1) * hg], 1, 2))
                                   for kk in range(2)]).astype(BF16)
        perm = _permutation()
        y_ssm = _s5(u, jnp.asarray(perm, BF16), jnp.asarray(perm.T, BF16),
                    cols(ssm_lam_re), cols(ssm_lam_im), logdt, b_t(ssm_b_re), b_t(ssm_b_im),
                    c_t(ssm_c_re), c_t(ssm_c_im), vec(ssm_d), bf(ssm_w_glu[i]), vec(ssm_b_glu))

        y_att = _attn(qt, k, vt)

        h = _out_ffn2(h1, y_ssm, y_att, p[i], vec(out_norm_ssm), vec(out_norm_attn),
                      bf(w_out[i][:SSM_WIDTH]), bf(w_out[i][SSM_WIDTH:]), vec(norm_ffn2),
                      bf(ffn2_w_gate[i]), bf(ffn2_w_up[i]), bf(ffn2_w_down[i]), vec(norm_ple),
                      bf(ple_w_gate[i]), bf(ple_w_proj[i]))
    return h.astype(x.dtype)
```

```python
import math

import jax
import jax.numpy as jnp
import numpy as np
from jax import lax
from jax.experimental import pallas as pl
from jax.experimental.pallas import tpu as pltpu

F32 = jnp.float32
BF16 = jnp.bfloat16

D_MODEL = 1024
BATCH = 8
SEQ = 4096
PLE_DIM = 256
D_FF = 2816
EPS = 1e-6
SSM_WIDTH = 512
SSM_GROUP = 16
SSM_GROUPS = 32
SSM_STATE = 64
MLA_HEADS = 8
MLA_NOPE = 64
MLA_ROPE = 32
MLA_QK = 96
MLA_V = 64
Q_LORA = 384
KV_LORA = 256
ROPE_THETA = 10000.0
IN_WIDTH = SSM_WIDTH + Q_LORA + KV_LORA + MLA_ROPE

LANES = 128
SUBLANES = 8
HEAD_PAD = LANES
QK_PAD = MLA_HEADS * HEAD_PAD
IN_PAD = 1280
MIX_PAD = IN_PAD - SSM_WIDTH
ROW_TILE = 512
ROW_TILES = SEQ // ROW_TILE
FF_CHUNKS = ((0, 1024), (1024, 1024), (2048, 768))
SSM_HALF_STATES = SSM_GROUPS * SSM_STATE // 2
SCAN_T = 32
SCAN_ROWS = SCAN_T * BATCH
ATT_TQ = 512
ATT_TK = 256
ATT_EXTRA = 16
VMEM_LIMIT = 56 * 1024 * 1024
NT_DIMS = (((1,), (1,)), ((), ()))


def _const_spec(shape):
    nd = len(shape)
    return pl.BlockSpec(shape, lambda *_: (0,) * nd, pipeline_mode=pl.Buffered(1))


def _rms(x, g):
    ms = jnp.mean(x * x, axis=-1, keepdims=True)
    return x * lax.rsqrt(ms + EPS) * g


def _dot(a, b):
    return jnp.dot(a, b, preferred_element_type=F32)


def _dot_nt(a, b):
    return lax.dot_general(a, b, NT_DIMS, preferred_element_type=F32)


def _swiglu_chunk(n, wg_ref, wu_ref, wd_ref, c0, cw):
    g = _dot(n, wg_ref[:, c0:c0 + cw])
    u = _dot(n, wu_ref[:, c0:c0 + cw])
    a = (g * jax.nn.sigmoid(g) * u).astype(BF16)
    return _dot(a, wd_ref[c0:c0 + cw, :])


def _swiglu(n, wg_ref, wu_ref, wd_ref, between=None):
    acc = None
    for i, (c0, cw) in enumerate(FF_CHUNKS):
        d = _swiglu_chunk(n, wg_ref, wu_ref, wd_ref, c0, cw)
        acc = d if acc is None else acc + d
        if between is not None:
            between[i]()
    return acc


def _ffn1_mix_kernel(x_ref, posr_ref, g1_ref, wg_ref, wu_ref, wd_ref, gmix_ref, win_ref,
                     gq_ref, wuqt_ref, gkv_ref, wukt_ref, wvt_ref, gqkq_ref, gqkk_ref,
                     invc_ref, eye_ref,
                     h_ref, u_ref, qt_ref, k_ref, vt_ref, z_ref, kt_ref):
    half = MLA_ROPE // 2

    @pl.when(pl.program_id(0) == 0)
    def _no_previous_tile():
        z_ref[...] = jnp.zeros_like(z_ref)

    def norm_rope_heads(src, shared_rope, gain_t, cos_t, sin_t, dst_ref, lo, hi):
        for hd in range(lo, hi):
            t = src[hd * HEAD_PAD:(hd + 1) * HEAD_PAD, :]
            if shared_rope is not None:
                t = t + shared_rope
            ss = jnp.sum(t * t, axis=0, keepdims=True)
            t = t * lax.rsqrt(ss * (1.0 / MLA_QK) + EPS) * gain_t
            x1 = t[MLA_NOPE:MLA_NOPE + half, :]
            x2 = t[MLA_NOPE + half:MLA_QK, :]
            r0 = hd * HEAD_PAD
            dst_ref[r0:r0 + MLA_NOPE, :] = t[:MLA_NOPE, :].astype(BF16)
            dst_ref[r0 + MLA_NOPE:r0 + MLA_NOPE + half, :] = (x1 * cos_t - x2 * sin_t).astype(BF16)
            dst_ref[r0 + MLA_NOPE + half:r0 + MLA_QK, :] = (x1 * sin_t + x2 * cos_t).astype(BF16)
            dst_ref[r0 + MLA_QK:r0 + HEAD_PAD, :] = t[MLA_QK:, :].astype(BF16)

    carried = {}

    def q_part(lo, hi):
        norm_rope_heads(carried["qt"], None, carried["gq_t"], carried["cos_t"],
                        carried["sin_t"], qt_ref, lo, hi)

    def k_part(lo, hi):
        norm_rope_heads(carried["kt"], carried["kpe_t"], carried["gk_t"], carried["cos_t"],
                        carried["sin_t"], kt_ref, lo, hi)

    def mix_part0():
        c_q = _rms(z_ref[:, :Q_LORA], gq_ref[...]).astype(BF16)
        c_kv = _rms(z_ref[:, Q_LORA:Q_LORA + KV_LORA], gkv_ref[...]).astype(BF16)
        vt = _dot_nt(wvt_ref[...], c_kv)
        for c in range(ROW_TILE // ATT_TK):
            vt_ref[c] = vt[:, c * ATT_TK:(c + 1) * ATT_TK].astype(BF16)
        carried["qt"] = _dot_nt(wuqt_ref[...], c_q)
        carried["kt"] = _dot_nt(wukt_ref[...], c_kv)
        carried["kpe_t"] = z_ref[:, Q_LORA + KV_LORA:].T
        ang_t = invc_ref[...] * posr_ref[...]
        carried["cos_t"] = jnp.cos(ang_t)
        carried["sin_t"] = jnp.sin(ang_t)
        carried["gq_t"] = jnp.broadcast_to(
            gqkq_ref[...] * (MLA_QK ** -0.5 * math.log2(math.e)), (HEAD_PAD, ROW_TILE))
        carried["gk_t"] = jnp.broadcast_to(gqkk_ref[...], (HEAD_PAD, ROW_TILE))
        q_part(0, MLA_HEADS // 2)

    def mix_part1():
        q_part(MLA_HEADS // 2, MLA_HEADS)
        k_part(0, MLA_HEADS // 2)

    def mix_part2():
        k_part(MLA_HEADS // 2, MLA_HEADS)
        k_ref[...] = _dot_nt(eye_ref[...], kt_ref[...]).astype(BF16)

    x = x_ref[...]
    h = x + 0.5 * _swiglu(_rms(x, g1_ref[...]).astype(BF16), wg_ref, wu_ref, wd_ref,
                          between=(mix_part0, mix_part1, mix_part2))
    h_ref[...] = h
    z = _dot(_rms(h, gmix_ref[...]).astype(BF16), win_ref[...])
    u_ref[...] = z[:, :SSM_WIDTH]
    z_ref[...] = z[:, SSM_WIDTH:]


def _ffn1_mix(x, posr, *consts):
    tm = ROW_TILE
    n_tiles = BATCH * ROW_TILES

    def cur(s):
        t = jnp.minimum(s, n_tiles - 1)
        return t // ROW_TILES, t % ROW_TILES

    def prev(s):
        t = jnp.maximum(s - 1, 0)
        return t // ROW_TILES, t % ROW_TILES

    def row(w, which):
        return pl.BlockSpec((None, tm, w), lambda s: (*which(s), 0))

    col_prev = lambda rows: pl.BlockSpec((None, rows, tm), lambda s: (prev(s)[0], 0, prev(s)[1]))
    return pl.pallas_call(
        _ffn1_mix_kernel,
        grid=(n_tiles + 1,),
        in_specs=[row(D_MODEL, cur), col_prev(1)] + [_const_spec(c.shape) for c in consts],
        out_specs=[row(D_MODEL, cur), row(SSM_WIDTH, cur), col_prev(QK_PAD), row(QK_PAD, prev),
                   pl.BlockSpec((None, tm // ATT_TK, MLA_HEADS * MLA_V, ATT_TK),
                                lambda s: (*prev(s), 0, 0))],
        out_shape=[jax.ShapeDtypeStruct((BATCH, SEQ, D_MODEL), F32),
                   jax.ShapeDtypeStruct((BATCH, SEQ, SSM_WIDTH), F32),
                   jax.ShapeDtypeStruct((BATCH, QK_PAD, SEQ), BF16),
                   jax.ShapeDtypeStruct((BATCH, SEQ, QK_PAD), BF16),
                   jax.ShapeDtypeStruct((BATCH, SEQ // ATT_TK, MLA_HEADS * MLA_V, ATT_TK),
                                        BF16)],
        scratch_shapes=[pltpu.VMEM((tm, MIX_PAD), F32),
                        pltpu.VMEM((QK_PAD, tm), BF16)],
        compiler_params=pltpu.CompilerParams(
            dimension_semantics=("arbitrary",), vmem_limit_bytes=VMEM_LIMIT),
        name="ffn1_mix",
    )(x, posr, *consts)


def _s5_kernel(u_ref, unext_ref, perm_ref, permt_ref, lam_re_ref, lam_im_ref, logdt_ref,
               bre_ref, bim_ref, wcr_ref, wci_ref, d_ref, wglu_ref, bglu_ref,
               y_ref, wb_ref, lbar_ref, state_ref, bua_ref, bub_ref, sa_ref, sb_ref):
    nh = SSM_HALF_STATES

    def b_project(u_bt, bu_ref):
        u_tb = _dot(perm_ref[...], u_bt.astype(BF16)).astype(BF16)
        half_in = SSM_WIDTH // 2
        for k in range(2):
            bu_ref[:, k * 2 * nh:(k + 1) * 2 * nh] = _dot(u_tb[:, k * half_in:(k + 1) * half_in],
                                                          wb_ref[k])

    def chunk(ref, c):
        return ref[:, c * SCAN_T:(c + 1) * SCAN_T, :].reshape(SCAN_ROWS, SSM_WIDTH)

    @pl.when(pl.program_id(0) == 0)
    def _prepare():
        lr = lam_re_ref[...]
        li = lam_im_ref[...]
        dt = jnp.exp(logdt_ref[...])
        mag = jnp.exp(lr * dt)
        br = mag * jnp.cos(li * dt)
        bi = mag * jnp.sin(li * dt)
        den = lr * lr + li * li
        cr = ((br - 1.0) * lr + bi * li) / den
        ci = (bi * lr - (br - 1.0) * li) / den
        for k in range(2):
            ks = slice(k * nh, (k + 1) * nh)
            lbar_ref[k, 0:1, :] = br[:, ks]
            lbar_ref[k, 1:2, :] = bi[:, ks]
            wb_ref[k, :, :nh] = (bre_ref[k] * cr[:, ks] - bim_ref[k] * ci[:, ks]).astype(BF16)
            wb_ref[k, :, nh:] = (bre_ref[k] * ci[:, ks] + bim_ref[k] * cr[:, ks]).astype(BF16)
        state_ref[...] = jnp.zeros_like(state_ref)
        b_project(chunk(u_ref, 0), bua_ref)

    def scan(bu_ref, s_ref):
        for k in range(2):
            c_re = k * 2 * nh
            c_im = c_re + nh
            lr = jnp.broadcast_to(lbar_ref[k, 0:1, :], (SUBLANES, nh))
            li = jnp.broadcast_to(lbar_ref[k, 1:2, :], (SUBLANES, nh))
            sr = state_ref[k, 0]
            si = state_ref[k, 1]
            for t2 in range(SCAN_T // 2):
                packed_r, packed_i = [], []
                for r in (t2 * 2 * SUBLANES, (t2 * 2 + 1) * SUBLANES):
                    nr = lr * sr - li * si + bu_ref[r:r + SUBLANES, c_re:c_re + nh]
                    ni = lr * si + li * sr + bu_ref[r:r + SUBLANES, c_im:c_im + nh]
                    sr, si = nr, ni
                    packed_r.append(sr)
                    packed_i.append(si)
                rows = slice(t2 * 2 * SUBLANES, (t2 + 1) * 2 * SUBLANES)
                s_ref[rows, c_re:c_re + nh] = jnp.concatenate(packed_r, axis=0).astype(BF16)
                s_ref[rows, c_im:c_im + nh] = jnp.concatenate(packed_i, axis=0).astype(BF16)
            state_ref[k, 0] = sr
            state_ref[k, 1] = si

    def c_project(s_ref, u_bt, c):
        ys = []
        for k in range(2):
            c_re = k * 2 * nh
            c_im = c_re + nh
            ys.append(_dot(s_ref[:, c_re:c_re + nh], wcr_ref[k])
                      - _dot(s_ref[:, c_im:c_im + nh], wci_ref[k]))
        y_tb = jnp.concatenate(ys, axis=1).astype(BF16)
        y = _dot(permt_ref[...], y_tb) + d_ref[...] * u_bt
        y = jax.nn.gelu(y)
        y = y * jax.nn.sigmoid(_dot(y.astype(BF16), wglu_ref[...]) + bglu_ref[...])
        y_ref[:, c * SCAN_T:(c + 1) * SCAN_T, :] = y.reshape(BATCH, SCAN_T, SSM_WIDTH)

    b_project(chunk(u_ref, 1), bub_ref)
    scan(bua_ref, sa_ref)
    b_project(chunk(unext_ref, 0), bua_ref)
    scan(bub_ref, sb_ref)
    c_project(sa_ref, chunk(u_ref, 0), 0)
    c_project(sb_ref, chunk(u_ref, 1), 1)


def _s5(u, *consts):
    nh = SSM_HALF_STATES
    n_chunks = SEQ // SCAN_T
    blk = pl.BlockSpec((BATCH, 2 * SCAN_T, SSM_WIDTH), lambda g: (0, g, 0))
    nxt = pl.BlockSpec((BATCH, SCAN_T, SSM_WIDTH),
                       lambda g: (0, jnp.minimum(2 * g + 2, n_chunks - 1), 0))
    return pl.pallas_call(
        _s5_kernel,
        grid=(n_chunks // 2,),
        in_specs=[blk, nxt] + [_const_spec(c.shape) for c in consts],
        out_specs=blk,
        out_shape=jax.ShapeDtypeStruct((BATCH, SEQ, SSM_WIDTH), F32),
        scratch_shapes=[
            pltpu.VMEM((2, SSM_WIDTH // 2, 2 * nh), BF16),
            pltpu.VMEM((2, 2, nh), F32),
            pltpu.VMEM((2, 2, SUBLANES, nh), F32),
            pltpu.VMEM((SCAN_ROWS, 4 * nh), F32),
            pltpu.VMEM((SCAN_ROWS, 4 * nh), F32),
            pltpu.VMEM((SCAN_ROWS, 4 * nh), BF16),
            pltpu.VMEM((SCAN_ROWS, 4 * nh), BF16),
        ],
        compiler_params=pltpu.CompilerParams(
            dimension_semantics=("arbitrary",), vmem_limit_bytes=VMEM_LIMIT),
        name="s5",
    )(u, u, *consts)


def _attn_kernel(qt_ref, k_ref, vt_ref, o_ref, m_ref, acc_ref, sta_ref, stb_ref):
    qi = pl.program_id(1)
    tq, tk = ATT_TQ, ATT_TK
    kpos = lax.broadcasted_iota(jnp.int32, (tk, tq), 0)
    qpos = lax.broadcasted_iota(jnp.int32, (tk, tq), 1)
    ones_rows = (lax.broadcasted_iota(jnp.int32, (ATT_EXTRA, tk), 0) == 0).astype(BF16)
    m_ref[...] = jnp.full(m_ref.shape, -jnp.inf, F32)
    acc_ref[...] = jnp.zeros(acc_ref.shape, F32)

    def scores(j, st_ref):
        r0 = pl.multiple_of(j * tk, tk)
        for hd in range(MLA_HEADS):
            st_ref[hd] = _dot(k_ref[pl.ds(r0, tk), hd * HEAD_PAD:(hd + 1) * HEAD_PAD],
                              qt_ref[hd * HEAD_PAD:(hd + 1) * HEAD_PAD, :])

    def consume(j, st_ref, visible=None):
        for hd in range(MLA_HEADS):
            st = st_ref[hd]
            if visible is not None:
                st = jnp.where(visible, st, -jnp.inf)
            m_old = m_ref[hd]
            m_new = jnp.maximum(m_old, jnp.max(st, axis=0, keepdims=True))
            alpha = jnp.exp2(m_old - m_new)
            pt = jnp.exp2(st - m_new).astype(BF16)
            vt = jnp.concatenate([vt_ref[j, hd * MLA_V:(hd + 1) * MLA_V, :], ones_rows], axis=0)
            acc_ref[hd] = alpha * acc_ref[hd] + _dot(vt, pt)
            m_ref[hd] = m_new

    def tile_pair(jj, carry):
        j = 2 * jj
        scores(j + 1, stb_ref)
        consume(j, sta_ref)
        scores(j + 2, sta_ref)
        consume(j + 1, stb_ref)
        return carry

    scores(0, sta_ref)
    lax.fori_loop(0, qi, tile_pair, 0)
    j = 2 * qi
    scores(j + 1, stb_ref)
    consume(j, sta_ref, kpos <= qpos)
    consume(j + 1, stb_ref, kpos + tk <= qpos)

    outs = [acc_ref[hd, :MLA_V, :] / acc_ref[hd, MLA_V:MLA_V + 1, :] for hd in range(MLA_HEADS)]
    per_slab = LANES * 2 // MLA_V
    for c in range(MLA_HEADS // per_slab):
        o_ref[:, c * per_slab * MLA_V:(c + 1) * per_slab * MLA_V] = jnp.concatenate(
            outs[c * per_slab:(c + 1) * per_slab], axis=0).T


def _attn(qt, k, vt):
    width = MLA_HEADS * MLA_V
    return pl.pallas_call(
        _attn_kernel,
        grid=(BATCH, SEQ // ATT_TQ),
        in_specs=[pl.BlockSpec((None, QK_PAD, ATT_TQ), lambda b, i: (b, 0, i)),
                  pl.BlockSpec((None, SEQ, QK_PAD), lambda b, i: (b, 0, 0)),
                  pl.BlockSpec((None, SEQ // ATT_TK, width, ATT_TK), lambda b, i: (b, 0, 0, 0))],
        out_specs=pl.BlockSpec((None, ATT_TQ, width), lambda b, i: (b, i, 0)),
        out_shape=jax.ShapeDtypeStruct((BATCH, SEQ, width), F32),
        scratch_shapes=[pltpu.VMEM((MLA_HEADS, 1, ATT_TQ), F32),
                        pltpu.VMEM((MLA_HEADS, MLA_V + ATT_EXTRA, ATT_TQ), F32),
                        pltpu.VMEM((MLA_HEADS, ATT_TK, ATT_TQ), F32),
                        pltpu.VMEM((MLA_HEADS, ATT_TK, ATT_TQ), F32)],
        compiler_params=pltpu.CompilerParams(
            dimension_semantics=("parallel", "arbitrary"), vmem_limit_bytes=VMEM_LIMIT),
        name="attn",
    )(qt, k, vt)


def _out_ffn2_kernel(h_ref, ys_ref, ya_ref, p_ref, gs_ref, ga_ref, wos_ref, woa_ref,
                     g2_ref, wg_ref, wu_ref, wd_ref, gple_ref, wgate_ref, wproj_ref, o_ref):
    h = (h_ref[...]
         + _dot(_rms(ys_ref[...], gs_ref[...]).astype(BF16), wos_ref[...])
         + _dot(_rms(ya_ref[...], ga_ref[...]).astype(BF16), woa_ref[...]))
    h = h + 0.5 * _swiglu(_rms(h, g2_ref[...]).astype(BF16), wg_ref, wu_ref, wd_ref)
    gate = jax.nn.sigmoid(_dot(_rms(h, gple_ref[...]).astype(BF16), wgate_ref[...]))
    o_ref[...] = h + gate * _dot(p_ref[...].astype(BF16), wproj_ref[...])


def _out_ffn2(h, ys, ya, p, *consts):
    tm = ROW_TILE
    row = lambda w: pl.BlockSpec((None, tm, w), lambda b, i: (b, i, 0))
    return pl.pallas_call(
        _out_ffn2_kernel,
        grid=(BATCH, ROW_TILES),
        in_specs=[row(D_MODEL), row(SSM_WIDTH), row(SSM_WIDTH), row(PLE_DIM)]
                 + [_const_spec(c.shape) for c in consts],
        out_specs=row(D_MODEL),
        out_shape=jax.ShapeDtypeStruct((BATCH, SEQ, D_MODEL), F32),
        compiler_params=pltpu.CompilerParams(
            dimension_semantics=("parallel", "parallel"), vmem_limit_bytes=VMEM_LIMIT),
        name="out_ffn2",
    )(h, ys, ya, p, *consts)


def _pad_heads(w, width):
    w = w.reshape(w.shape[0], MLA_HEADS, width)
    return jnp.pad(w, ((0, 0), (0, 0), (0, HEAD_PAD - width))).reshape(w.shape[0], QK_PAD)


def _block_diag(blocks):
    g, r, c = blocks.shape
    eye = jnp.eye(g, dtype=blocks.dtype)
    return (blocks[:, :, None, :] * eye[:, None, :, None]).reshape(g * r, g * c)


def _permutation():
    rows = np.arange(SCAN_ROWS)
    t, b = rows // BATCH, rows % BATCH
    perm = np.zeros((SCAN_ROWS, SCAN_ROWS), np.float32)
    perm[rows, b * SCAN_T + t] = 1.0
    return perm


def kernel(x, p, positions, norm_ffn1, ffn1_w_gate, ffn1_w_up, ffn1_w_down, norm_mix, w_in,
           ssm_lam_re, ssm_lam_im, ssm_log_dt, ssm_b_re, ssm_b_im, ssm_c_re, ssm_c_im, ssm_d,
           ssm_w_glu, ssm_b_glu, mla_q_norm, mla_w_uq, mla_kv_norm, mla_w_ukv, mla_qk_norm_q,
           mla_qk_norm_k, out_norm_ssm, out_norm_attn, w_out, norm_ffn2, ffn2_w_gate, ffn2_w_up,
           ffn2_w_down, norm_ple, ple_w_gate, ple_w_proj):
    h = x
    for i in range(norm_ffn1.shape[0]):
        vec = lambda a: a[i].reshape(1, -1).astype(F32)
        bf = lambda a: a.astype(BF16)

        pos = positions.astype(F32)
        win = jnp.zeros((D_MODEL, IN_PAD), F32)
        win = win.at[:, :IN_WIDTH - MLA_ROPE].set(w_in[i][:, :IN_WIDTH - MLA_ROPE])
        kpe0 = IN_WIDTH - MLA_ROPE + MLA_NOPE
        win = win.at[:, kpe0:kpe0 + MLA_ROPE].set(w_in[i][:, IN_WIDTH - MLA_ROPE:])
        wuqt = _pad_heads(mla_w_uq[i], MLA_QK).T
        wukv = mla_w_ukv[i].reshape(KV_LORA, MLA_HEADS, MLA_NOPE + MLA_V)
        wukt = _pad_heads(wukv[..., :MLA_NOPE].reshape(KV_LORA, -1), MLA_NOPE).T
        wvt = wukv[..., MLA_NOPE:].reshape(KV_LORA, -1).T
        pad_gain = lambda g: jnp.pad(g[i].astype(F32), (0, HEAD_PAD - MLA_QK)).reshape(HEAD_PAD, 1)
        half = MLA_ROPE // 2
        inv = ROPE_THETA ** (-jnp.arange(half, dtype=F32) / half)

        h1, u, qt, k, vt = _ffn1_mix(
            h, pos[:, None, :],
            vec(norm_ffn1), bf(ffn1_w_gate[i]), bf(ffn1_w_up[i]), bf(ffn1_w_down[i]),
            vec(norm_mix), bf(win), vec(mla_q_norm), bf(wuqt), vec(mla_kv_norm), bf(wukt), bf(wvt),
            pad_gain(mla_qk_norm_q), pad_gain(mla_qk_norm_k), inv.reshape(half, 1),
            jnp.eye(ROW_TILE, dtype=BF16))

        hg = SSM_GROUPS // 2
        cols = lambda a: a[i].astype(F32).reshape(1, SSM_GROUPS * SSM_STATE)
        logdt = jnp.repeat(ssm_log_dt[i].astype(F32), SSM_STATE).reshape(1, -1)
        b_t = lambda b: jnp.stack([_block_diag(jnp.swapaxes(b[i][kk * hg:(kk + 1) * hg], 1, 2))
                                   for kk in range(2)]).astype(F32)
        c_t = lambda c: jnp.stack([_block_diag(jnp.swapaxes(c[i][kk * hg:(kk + 1) * hg], 1, 2))
                                   for kk in range(2)]).astype(BF16)
        perm = _permutation()
        y_ssm = _s5(u, jnp.asarray(perm, BF16), jnp.asarray(perm.T, BF16),
                    cols(ssm_lam_re), cols(ssm_lam_im), logdt, b_t(ssm_b_re), b_t(ssm_b_im),
                    c_t(ssm_c_re), c_t(ssm_c_im), vec(ssm_d), bf(ssm_w_glu[i]), vec(ssm_b_glu))

        y_att = _attn(qt, k, vt)

        h = _out_ffn2(h1, y_ssm, y_att, p[i], vec(out_norm_ssm), vec(out_norm_attn),
                      bf(w_out[i][:SSM_WIDTH]), bf(w_out[i][SSM_WIDTH:]), vec(norm_ffn2),
                      bf(ffn2_w_gate[i]), bf(ffn2_w_up[i]), bf(ffn2_w_down[i]), vec(norm_ple),
                      bf(ple_w_gate[i]), bf(ple_w_proj[i]))
    return h.astype(x.dtype)
```

```python
import math

import jax
import jax.numpy as jnp
import numpy as np
from jax import lax
from jax.experimental import pallas as pl
from jax.experimental.pallas import tpu as pltpu

F32 = jnp.float32
BF16 = jnp.bfloat16

D_MODEL = 1024
BATCH = 8
SEQ = 4096
PLE_DIM = 256
D_FF = 2816
EPS = 1e-6
SSM_WIDTH = 512
SSM_GROUP = 16
SSM_GROUPS = 32
SSM_STATE = 64
MLA_HEADS = 8
MLA_NOPE = 64
MLA_ROPE = 32
MLA_QK = 96
MLA_V = 64
Q_LORA = 384
KV_LORA = 256
ROPE_THETA = 10000.0
IN_WIDTH = SSM_WIDTH + Q_LORA + KV_LORA + MLA_ROPE

LANES = 128
SUBLANES = 8
HEAD_PAD = LANES
QK_PAD = MLA_HEADS * HEAD_PAD
IN_PAD = 1280
MIX_PAD = IN_PAD - SSM_WIDTH
ROW_TILE = 512
ROW_TILES = SEQ // ROW_TILE
FF_CHUNKS = ((0, 1024), (1024, 1024), (2048, 768))
SSM_HALF_STATES = SSM_GROUPS * SSM_STATE // 2
SCAN_T = 32
SCAN_ROWS = SCAN_T * BATCH
ATT_TQ = 512
ATT_TK = 256
ATT_EXTRA = 16
VMEM_LIMIT = 56 * 1024 * 1024
NT_DIMS = (((1,), (1,)), ((), ()))


def _const_spec(shape):
    nd = len(shape)
    return pl.BlockSpec(shape, lambda *_: (0,) * nd, pipeline_mode=pl.Buffered(1))


def _rms(x, g):
    ms = jnp.mean(x * x, axis=-1, keepdims=True)
    return x * lax.rsqrt(ms + EPS) * g


def _dot(a, b):
    return jnp.dot(a, b, preferred_element_type=F32)


def _dot_nt(a, b):
    return lax.dot_general(a, b, NT_DIMS, preferred_element_type=F32)


def _swiglu_chunk(n, wg_ref, wu_ref, wd_ref, c0, cw):
    g = _dot(n, wg_ref[:, c0:c0 + cw])
    u = _dot(n, wu_ref[:, c0:c0 + cw])
    a = (g * jax.nn.sigmoid(g) * u).astype(BF16)
    return _dot(a, wd_ref[c0:c0 + cw, :])


def _ffn1_mix_kernel(x_ref, posr_ref, g1_ref, wg_ref, wu_ref, wd_ref, gmix_ref, win_ref,
                     gq_ref, wuqt_ref, gkv_ref, wukt_ref, wvt_ref, gqkq_ref, gqkk_ref,
                     invc_ref, eye_ref,
                     h_ref, u_ref, qt_ref, k_ref, vt_ref, z_ref, kt_ref):
    half = MLA_ROPE // 2

    @pl.when(pl.program_id(0) == 0)
    def _no_previous_tile():
        z_ref[...] = jnp.zeros_like(z_ref)

    def norm_rope_heads(src, shared_rope, gain_t, cos_t, sin_t, dst_ref, lo, hi):
        for hd in range(lo, hi):
            t = src[hd * HEAD_PAD:(hd + 1) * HEAD_PAD, :]
            if shared_rope is not None:
                t = t + shared_rope
            ss = jnp.sum(t * t, axis=0, keepdims=True)
            t = t * lax.rsqrt(ss * (1.0 / MLA_QK) + EPS) * gain_t
            x1 = t[MLA_NOPE:MLA_NOPE + half, :]
            x2 = t[MLA_NOPE + half:MLA_QK, :]
            r0 = hd * HEAD_PAD
            dst_ref[r0:r0 + MLA_NOPE, :] = t[:MLA_NOPE, :].astype(BF16)
            dst_ref[r0 + MLA_NOPE:r0 + MLA_NOPE + half, :] = (x1 * cos_t - x2 * sin_t).astype(BF16)
            dst_ref[r0 + MLA_NOPE + half:r0 + MLA_QK, :] = (x1 * sin_t + x2 * cos_t).astype(BF16)
            dst_ref[r0 + MLA_QK:r0 + HEAD_PAD, :] = t[MLA_QK:, :].astype(BF16)

    carried = {}

    def q_part(lo, hi):
        norm_rope_heads(carried["qt"], None, carried["gq_t"], carried["cos_t"],
                        carried["sin_t"], qt_ref, lo, hi)

    def k_part(lo, hi):
        norm_rope_heads(carried["kt"], carried["kpe_t"], carried["gk_t"], carried["cos_t"],
                        carried["sin_t"], kt_ref, lo, hi)

    def mix_part0():
        c_q = _rms(z_ref[:, :Q_LORA], gq_ref[...]).astype(BF16)
        c_kv = _rms(z_ref[:, Q_LORA:Q_LORA + KV_LORA], gkv_ref[...]).astype(BF16)
        vt = _dot_nt(wvt_ref[...], c_kv)
        for c in range(ROW_TILE // ATT_TK):
            vt_ref[c] = vt[:, c * ATT_TK:(c + 1) * ATT_TK].astype(BF16)
        carried["qt"] = _dot_nt(wuqt_ref[...], c_q)
        carried["kt"] = _dot_nt(wukt_ref[...], c_kv)
        carried["kpe_t"] = z_ref[:, Q_LORA + KV_LORA:].T
        ang_t = invc_ref[...] * posr_ref[...]
        carried["cos_t"] = jnp.cos(ang_t)
        carried["sin_t"] = jnp.sin(ang_t)
        carried["gq_t"] = jnp.broadcast_to(
            gqkq_ref[...] * (MLA_QK ** -0.5 * math.log2(math.e)), (HEAD_PAD, ROW_TILE))
        carried["gk_t"] = jnp.broadcast_to(gqkk_ref[...], (HEAD_PAD, ROW_TILE))
        q_part(0, MLA_HEADS // 2)

    def mix_part1():
        q_part(MLA_HEADS // 2, MLA_HEADS)
        k_part(0, MLA_HEADS // 2)

    def mix_part2():
        k_part(MLA_HEADS // 2, MLA_HEADS)
        k_ref[...] = _dot_nt(eye_ref[...], kt_ref[...]).astype(BF16)

    halves = [slice(r * (ROW_TILE // 2), (r + 1) * (ROW_TILE // 2)) for r in range(2)]
    ns = [_rms(x_ref[rows, :], g1_ref[...]).astype(BF16) for rows in halves]
    accs = [None, None]
    for (c0, cw), mix_part in zip(FF_CHUNKS, (mix_part0, mix_part1, mix_part2)):
        for r in range(2):
            d = _swiglu_chunk(ns[r], wg_ref, wu_ref, wd_ref, c0, cw)
            accs[r] = d if accs[r] is None else accs[r] + d
        mix_part()
    for r, rows in enumerate(halves):
        h = x_ref[rows, :] + 0.5 * accs[r]
        h_ref[rows, :] = h
        z = _dot(_rms(h, gmix_ref[...]).astype(BF16), win_ref[...])
        u_ref[rows, :] = z[:, :SSM_WIDTH]
        z_ref[rows, :] = z[:, SSM_WIDTH:]


def _ffn1_mix(x, posr, *consts):
    tm = ROW_TILE
    n_tiles = BATCH * ROW_TILES

    def cur(s):
        t = jnp.minimum(s, n_tiles - 1)
        return t // ROW_TILES, t % ROW_TILES

    def prev(s):
        t = jnp.maximum(s - 1, 0)
        return t // ROW_TILES, t % ROW_TILES

    def row(w, which):
        return pl.BlockSpec((None, tm, w), lambda s: (*which(s), 0))

    col_prev = lambda rows: pl.BlockSpec((None, rows, tm), lambda s: (prev(s)[0], 0, prev(s)[1]))
    return pl.pallas_call(
        _ffn1_mix_kernel,
        grid=(n_tiles + 1,),
        in_specs=[row(D_MODEL, cur), col_prev(1)] + [_const_spec(c.shape) for c in consts],
        out_specs=[row(D_MODEL, cur), row(SSM_WIDTH, cur), col_prev(QK_PAD), row(QK_PAD, prev),
                   pl.BlockSpec((None, tm // ATT_TK, MLA_HEADS * MLA_V, ATT_TK),
                                lambda s: (*prev(s), 0, 0))],
        out_shape=[jax.ShapeDtypeStruct((BATCH, SEQ, D_MODEL), F32),
                   jax.ShapeDtypeStruct((BATCH, SEQ, SSM_WIDTH), F32),
                   jax.ShapeDtypeStruct((BATCH, QK_PAD, SEQ), BF16),
                   jax.ShapeDtypeStruct((BATCH, SEQ, QK_PAD), BF16),
                   jax.ShapeDtypeStruct((BATCH, SEQ // ATT_TK, MLA_HEADS * MLA_V, ATT_TK),
                                        BF16)],
        scratch_shapes=[pltpu.VMEM((tm, MIX_PAD), F32),
                        pltpu.VMEM((QK_PAD, tm), BF16)],
        compiler_params=pltpu.CompilerParams(
            dimension_semantics=("arbitrary",), vmem_limit_bytes=VMEM_LIMIT),
        name="ffn1_mix",
    )(x, posr, *consts)


def _s5_kernel(u_ref, unext_ref, perm_ref, permt_ref, lam_re_ref, lam_im_ref, logdt_ref,
               bre_ref, bim_ref, wcr_ref, wci_ref, d_ref, wglu_ref, bglu_ref,
               y_ref, wb_ref, lbar_ref, state_ref, bua_ref, bub_ref, sa_ref, sb_ref):
    nh = SSM_HALF_STATES

    def b_project(u_bt, bu_ref):
        u_tb = _dot(perm_ref[...], u_bt.astype(BF16)).astype(BF16)
        half_in = SSM_WIDTH // 2
        for k in range(2):
            bu_ref[:, k * 2 * nh:(k + 1) * 2 * nh] = _dot(u_tb[:, k * half_in:(k + 1) * half_in],
                                                          wb_ref[k])

    def chunk(ref, c):
        return ref[:, c * SCAN_T:(c + 1) * SCAN_T, :].reshape(SCAN_ROWS, SSM_WIDTH)

    @pl.when(pl.program_id(0) == 0)
    def _prepare():
        lr = lam_re_ref[...]
        li = lam_im_ref[...]
        dt = jnp.exp(logdt_ref[...])
        mag = jnp.exp(lr * dt)
        br = mag * jnp.cos(li * dt)
        bi = mag * jnp.sin(li * dt)
        den = lr * lr + li * li
        cr = ((br - 1.0) * lr + bi * li) / den
        ci = (bi * lr - (br - 1.0) * li) / den
        for k in range(2):
            ks = slice(k * nh, (k + 1) * nh)
            lbar_ref[k, 0:1, :] = br[:, ks]
            lbar_ref[k, 1:2, :] = bi[:, ks]
            wb_ref[k, :, :nh] = (bre_ref[k] * cr[:, ks] - bim_ref[k] * ci[:, ks]).astype(BF16)
            wb_ref[k, :, nh:] = (bre_ref[k] * ci[:, ks] + bim_ref[k] * cr[:, ks]).astype(BF16)
        state_ref[...] = jnp.zeros_like(state_ref)
        b_project(chunk(u_ref, 0), bua_ref)

    def scan(bu_ref, s_ref):
        for k in range(2):
            c_re = k * 2 * nh
            c_im = c_re + nh
            lr = jnp.broadcast_to(lbar_ref[k, 0:1, :], (SUBLANES, nh))
            li = jnp.broadcast_to(lbar_ref[k, 1:2, :], (SUBLANES, nh))
            sr = state_ref[k, 0]
            si = state_ref[k, 1]
            for t2 in range(SCAN_T // 2):
                packed_r, packed_i = [], []
                for r in (t2 * 2 * SUBLANES, (t2 * 2 + 1) * SUBLANES):
                    nr = lr * sr - li * si + bu_ref[r:r + SUBLANES, c_re:c_re + nh]
                    ni = lr * si + li * sr + bu_ref[r:r + SUBLANES, c_im:c_im + nh]
                    sr, si = nr, ni
                    packed_r.append(sr)
                    packed_i.append(si)
                rows = slice(t2 * 2 * SUBLANES, (t2 + 1) * 2 * SUBLANES)
                s_ref[rows, c_re:c_re + nh] = jnp.concatenate(packed_r, axis=0).astype(BF16)
                s_ref[rows, c_im:c_im + nh] = jnp.concatenate(packed_i, axis=0).astype(BF16)
            state_ref[k, 0] = sr
            state_ref[k, 1] = si

    def c_project(s_ref, u_bt, c):
        ys = []
        for k in range(2):
            c_re = k * 2 * nh
            c_im = c_re + nh
            ys.append(_dot(s_ref[:, c_re:c_re + nh], wcr_ref[k])
                      - _dot(s_ref[:, c_im:c_im + nh], wci_ref[k]))
        y_tb = jnp.concatenate(ys, axis=1).astype(BF16)
        y = _dot(permt_ref[...], y_tb) + d_ref[...] * u_bt
        y = jax.nn.gelu(y)
        y = y * jax.nn.sigmoid(_dot(y.astype(BF16), wglu_ref[...]) + bglu_ref[...])
        y_ref[:, c * SCAN_T:(c + 1) * SCAN_T, :] = y.reshape(BATCH, SCAN_T, SSM_WIDTH)

    b_project(chunk(u_ref, 1), bub_ref)
    scan(bua_ref, sa_ref)
    b_project(chunk(unext_ref, 0), bua_ref)
    scan(bub_ref, sb_ref)
    c_project(sa_ref, chunk(u_ref, 0), 0)
    c_project(sb_ref, chunk(u_ref, 1), 1)


def _s5(u, *consts):
    nh = SSM_HALF_STATES
    n_chunks = SEQ // SCAN_T
    blk = pl.BlockSpec((BATCH, 2 * SCAN_T, SSM_WIDTH), lambda g: (0, g, 0))
    nxt = pl.BlockSpec((BATCH, SCAN_T, SSM_WIDTH),
                       lambda g: (0, jnp.minimum(2 * g + 2, n_chunks - 1), 0))
    return pl.pallas_call(
        _s5_kernel,
        grid=(n_chunks // 2,),
        in_specs=[blk, nxt] + [_const_spec(c.shape) for c in consts],
        out_specs=blk,
        out_shape=jax.ShapeDtypeStruct((BATCH, SEQ, SSM_WIDTH), F32),
        scratch_shapes=[
            pltpu.VMEM((2, SSM_WIDTH // 2, 2 * nh), BF16),
            pltpu.VMEM((2, 2, nh), F32),
            pltpu.VMEM((2, 2, SUBLANES, nh), F32),
            pltpu.VMEM((SCAN_ROWS, 4 * nh), F32),
            pltpu.VMEM((SCAN_ROWS, 4 * nh), F32),
            pltpu.VMEM((SCAN_ROWS, 4 * nh), BF16),
            pltpu.VMEM((SCAN_ROWS, 4 * nh), BF16),
        ],
        compiler_params=pltpu.CompilerParams(
            dimension_semantics=("arbitrary",), vmem_limit_bytes=VMEM_LIMIT),
        name="s5",
    )(u, u, *consts)


def _attn_kernel(qt_ref, k_ref, vt_ref, o_ref, m_ref, acc_ref, sta_ref, stb_ref):
    qi = pl.program_id(1)
    tq, tk = ATT_TQ, ATT_TK
    kpos = lax.broadcasted_iota(jnp.int32, (tk, tq), 0)
    qpos = lax.broadcasted_iota(jnp.int32, (tk, tq), 1)
    ones_rows = (lax.broadcasted_iota(jnp.int32, (ATT_EXTRA, tk), 0) == 0).astype(BF16)
    m_ref[...] = jnp.full(m_ref.shape, -jnp.inf, F32)
    acc_ref[...] = jnp.zeros(acc_ref.shape, F32)

    def scores(j, st_ref):
        r0 = pl.multiple_of(j * tk, tk)
        for hd in range(MLA_HEADS):
            st_ref[hd] = _dot(k_ref[pl.ds(r0, tk), hd * HEAD_PAD:(hd + 1) * HEAD_PAD],
                              qt_ref[hd * HEAD_PAD:(hd + 1) * HEAD_PAD, :])

    def consume(j, st_ref, visible=None):
        for hd in range(MLA_HEADS):
            st = st_ref[hd]
            if visible is not None:
                st = jnp.where(visible, st, -jnp.inf)
            m_old = m_ref[hd]
            m_new = jnp.maximum(m_old, jnp.max(st, axis=0, keepdims=True))
            alpha = jnp.exp2(m_old - m_new)
            pt = jnp.exp2(st - m_new).astype(BF16)
            vt = jnp.concatenate([vt_ref[j, hd * MLA_V:(hd + 1) * MLA_V, :], ones_rows], axis=0)
            acc_ref[hd] = alpha * acc_ref[hd] + _dot(vt, pt)
            m_ref[hd] = m_new

    def tile_pair(jj, carry):
        j = 2 * jj
        scores(j + 1, stb_ref)
        consume(j, sta_ref)
        scores(j + 2, sta_ref)
        consume(j + 1, stb_ref)
        return carry

    scores(0, sta_ref)
    lax.fori_loop(0, qi, tile_pair, 0)
    j = 2 * qi
    scores(j + 1, stb_ref)
    consume(j, sta_ref, kpos <= qpos)
    consume(j + 1, stb_ref, kpos + tk <= qpos)

    outs = [acc_ref[hd, :MLA_V, :] / acc_ref[hd, MLA_V:MLA_V + 1, :] for hd in range(MLA_HEADS)]
    per_slab = LANES * 2 // MLA_V
    for c in range(MLA_HEADS // per_slab):
        o_ref[:, c * per_slab * MLA_V:(c + 1) * per_slab * MLA_V] = jnp.concatenate(
            outs[c * per_slab:(c + 1) * per_slab], axis=0).T


def _attn(qt, k, vt):
    width = MLA_HEADS * MLA_V
    return pl.pallas_call(
        _attn_kernel,
        grid=(BATCH, SEQ // ATT_TQ),
        in_specs=[pl.BlockSpec((None, QK_PAD, ATT_TQ), lambda b, i: (b, 0, i)),
                  pl.BlockSpec((None, SEQ, QK_PAD), lambda b, i: (b, 0, 0)),
                  pl.BlockSpec((None, SEQ // ATT_TK, width, ATT_TK), lambda b, i: (b, 0, 0, 0))],
        out_specs=pl.BlockSpec((None, ATT_TQ, width), lambda b, i: (b, i, 0)),
        out_shape=jax.ShapeDtypeStruct((BATCH, SEQ, width), F32),
        scratch_shapes=[pltpu.VMEM((MLA_HEADS, 1, ATT_TQ), F32),
                        pltpu.VMEM((MLA_HEADS, MLA_V + ATT_EXTRA, ATT_TQ), F32),
                        pltpu.VMEM((MLA_HEADS, ATT_TK, ATT_TQ), F32),
                        pltpu.VMEM((MLA_HEADS, ATT_TK, ATT_TQ), F32)],
        compiler_params=pltpu.CompilerParams(
            dimension_semantics=("parallel", "arbitrary"), vmem_limit_bytes=VMEM_LIMIT),
        name="attn",
    )(qt, k, vt)


def _out_ffn2_kernel(h_ref, ys_ref, ya_ref, p_ref, gs_ref, ga_ref, wos_ref, woa_ref,
                     g2_ref, wg_ref, wu_ref, wd_ref, gple_ref, wgate_ref, wproj_ref, o_ref):
    halves = [slice(r * (ROW_TILE // 2), (r + 1) * (ROW_TILE // 2)) for r in range(2)]
    hs, ns, accs = [], [], [None, None]
    for rows in halves:
        h = (h_ref[rows, :]
             + _dot(_rms(ys_ref[rows, :], gs_ref[...]).astype(BF16), wos_ref[...])
             + _dot(_rms(ya_ref[rows, :], ga_ref[...]).astype(BF16), woa_ref[...]))
        hs.append(h)
        ns.append(_rms(h, g2_ref[...]).astype(BF16))
    for c0, cw in FF_CHUNKS:
        for r in range(2):
            d = _swiglu_chunk(ns[r], wg_ref, wu_ref, wd_ref, c0, cw)
            accs[r] = d if accs[r] is None else accs[r] + d
    for r, rows in enumerate(halves):
        h = hs[r] + 0.5 * accs[r]
        gate = jax.nn.sigmoid(_dot(_rms(h, gple_ref[...]).astype(BF16), wgate_ref[...]))
        o_ref[rows, :] = h + gate * _dot(p_ref[rows, :].astype(BF16), wproj_ref[...])


def _out_ffn2(h, ys, ya, p, *consts):
    tm = ROW_TILE
    row = lambda w: pl.BlockSpec((None, tm, w), lambda b, i: (b, i, 0))
    return pl.pallas_call(
        _out_ffn2_kernel,
        grid=(BATCH, ROW_TILES),
        in_specs=[row(D_MODEL), row(SSM_WIDTH), row(SSM_WIDTH), row(PLE_DIM)]
                 + [_const_spec(c.shape) for c in consts],
        out_specs=row(D_MODEL),
        out_shape=jax.ShapeDtypeStruct((BATCH, SEQ, D_MODEL), F32),
        compiler_params=pltpu.CompilerParams(
            dimension_semantics=("parallel", "parallel"), vmem_limit_bytes=VMEM_LIMIT),
        name="out_ffn2",
    )(h, ys, ya, p, *consts)


def _pad_heads(w, width):
    w = w.reshape(w.shape[0], MLA_HEADS, width)
    return jnp.pad(w, ((0, 0), (0, 0), (0, HEAD_PAD - width))).reshape(w.shape[0], QK_PAD)


def _block_diag(blocks):
    g, r, c = blocks.shape
    eye = jnp.eye(g, dtype=blocks.dtype)
    return (blocks[:, :, None, :] * eye[:, None, :, None]).reshape(g * r, g * c)


def _permutation():
    rows = np.arange(SCAN_ROWS)
    t, b = rows // BATCH, rows % BATCH
    perm = np.zeros((SCAN_ROWS, SCAN_ROWS), np.float32)
    perm[rows, b * SCAN_T + t] = 1.0
    return perm


def kernel(x, p, positions, norm_ffn1, ffn1_w_gate, ffn1_w_up, ffn1_w_down, norm_mix, w_in,
           ssm_lam_re, ssm_lam_im, ssm_log_dt, ssm_b_re, ssm_b_im, ssm_c_re, ssm_c_im, ssm_d,
           ssm_w_glu, ssm_b_glu, mla_q_norm, mla_w_uq, mla_kv_norm, mla_w_ukv, mla_qk_norm_q,
           mla_qk_norm_k, out_norm_ssm, out_norm_attn, w_out, norm_ffn2, ffn2_w_gate, ffn2_w_up,
           ffn2_w_down, norm_ple, ple_w_gate, ple_w_proj):
    h = x
    for i in range(norm_ffn1.shape[0]):
        vec = lambda a: a[i].reshape(1, -1).astype(F32)
        bf = lambda a: a.astype(BF16)

        pos = positions.astype(F32)
        win = jnp.zeros((D_MODEL, IN_PAD), F32)
        win = win.at[:, :IN_WIDTH - MLA_ROPE].set(w_in[i][:, :IN_WIDTH - MLA_ROPE])
        kpe0 = IN_WIDTH - MLA_ROPE + MLA_NOPE
        win = win.at[:, kpe0:kpe0 + MLA_ROPE].set(w_in[i][:, IN_WIDTH - MLA_ROPE:])
        wuqt = _pad_heads(mla_w_uq[i], MLA_QK).T
        wukv = mla_w_ukv[i].reshape(KV_LORA, MLA_HEADS, MLA_NOPE + MLA_V)
        wukt = _pad_heads(wukv[..., :MLA_NOPE].reshape(KV_LORA, -1), MLA_NOPE).T
        wvt = wukv[..., MLA_NOPE:].reshape(KV_LORA, -1).T
        pad_gain = lambda g: jnp.pad(g[i].astype(F32), (0, HEAD_PAD - MLA_QK)).reshape(HEAD_PAD, 1)
        half = MLA_ROPE // 2
        inv = ROPE_THETA ** (-jnp.arange(half, dtype=F32) / half)

        h1, u, qt, k, vt = _ffn1_mix(
            h, pos[:, None, :],
            vec(norm_ffn1), bf(ffn1_w_gate[i]), bf(ffn1_w_up[i]), bf(ffn1_w_down[i]),
            vec(norm_mix), bf(win), vec(mla_q_norm), bf(wuqt), vec(mla_kv_norm), bf(wukt), bf(wvt),
            pad_gain(mla_qk_norm_q), pad_gain(mla_qk_norm_k), inv.reshape(half, 1),
            jnp.eye(ROW_TILE, dtype=BF16))

        hg = SSM_GROUPS // 2
        cols = lambda a: a[i].astype(F32).reshape(1, SSM_GROUPS * SSM_STATE)
        logdt = jnp.repeat(ssm_log_dt[i].astype(F32), SSM_STATE).reshape(1, -1)
        b_t = lambda b: jnp.stack([_block_diag(jnp.swapaxes(b[i][kk * hg:(kk + 1) * hg], 1, 2))
                                   for kk in range(2)]).astype(F32)
        c_t = lambda c: jnp.stack([_block_diag(jnp.swapaxes(c[i][kk * hg:(kk + 1) * hg], 1, 2))
                                   for kk in range(2)]).astype(BF16)
        perm = _permutation()
        y_ssm = _s5(u, jnp.asarray(perm, BF16), jnp.asarray(perm.T, BF16),
                    cols(ssm_lam_re), cols(ssm_lam_im), logdt, b_t(ssm_b_re), b_t(ssm_b_im),
                    c_t(ssm_c_re), c_t(ssm_c_im), vec(ssm_d), bf(ssm_w_glu[i]), vec(ssm_b_glu))

        y_att = _attn(qt, k, vt)

        h = _out_ffn2(h1, y_ssm, y_att, p[i], vec(out_norm_ssm), vec(out_norm_attn),
                      bf(w_out[i][:SSM_WIDTH]), bf(w_out[i][SSM_WIDTH:]), vec(norm_ffn2),
                      bf(ffn2_w_gate[i]), bf(ffn2_w_up[i]), bf(ffn2_w_down[i]), vec(norm_ple),
                      bf(ple_w_gate[i]), bf(ple_w_proj[i]))
    return h.astype(x.dtype)
```

```python
import math

import jax
import jax.numpy as jnp
import numpy as np
from jax import lax
from jax.experimental import pallas as pl
from jax.experimental.pallas import tpu as pltpu

F32 = jnp.float32
BF16 = jnp.bfloat16

D_MODEL = 1024
BATCH = 8
SEQ = 4096
PLE_DIM = 256
D_FF = 2816
EPS = 1e-6
SSM_WIDTH = 512
SSM_GROUP = 16
SSM_GROUPS = 32
SSM_STATE = 64
MLA_HEADS = 8
MLA_NOPE = 64
MLA_ROPE = 32
MLA_QK = 96
MLA_V = 64
Q_LORA = 384
KV_LORA = 256
ROPE_THETA = 10000.0
IN_WIDTH = SSM_WIDTH + Q_LORA + KV_LORA + MLA_ROPE

LANES = 128
SUBLANES = 8
HEAD_PAD = LANES
QK_PAD = MLA_HEADS * HEAD_PAD
IN_PAD = 1280
MIX_PAD = IN_PAD - SSM_WIDTH
ROW_TILE = 512
ROW_TILES = SEQ // ROW_TILE
ROW_GROUPS = 2
FF_CHUNKS = ((0, 1024), (1024, 1024), (2048, 768))
SSM_HALF_STATES = SSM_GROUPS * SSM_STATE // 2
SCAN_T = 32
SCAN_ROWS = SCAN_T * BATCH
ATT_TQ = 512
ATT_TK = 256
ATT_EXTRA = 16
CAST_STEPS = 8
VMEM_LIMIT = 56 * 1024 * 1024
NT_DIMS = (((1,), (1,)), ((), ()))


def _const_spec(shape):
    nd = len(shape)
    return pl.BlockSpec(shape, lambda *_: (0,) * nd, pipeline_mode=pl.Buffered(1))


def _rms(x, g):
    ms = jnp.mean(x * x, axis=-1, keepdims=True)
    return x * lax.rsqrt(ms + EPS) * g


def _dot(a, b):
    return jnp.dot(a, b, preferred_element_type=F32)


def _dot_nt(a, b):
    return lax.dot_general(a, b, NT_DIMS, preferred_element_type=F32)


def _row_groups():
    rows = ROW_TILE // ROW_GROUPS
    return [slice(r * rows, (r + 1) * rows) for r in range(ROW_GROUPS)]


def _swiglu_chunk(n, wg_ref, wu_ref, wd_ref, c0, cw):
    g = _dot(n, wg_ref[:, c0:c0 + cw])
    u = _dot(n, wu_ref[:, c0:c0 + cw])
    a = (g * jax.nn.sigmoid(g) * u).astype(BF16)
    return _dot(a, wd_ref[c0:c0 + cw, :])


def _cast_kernel(*refs):
    n = len(refs) // 2
    for src, dst in zip(refs[:n], refs[n:]):
        dst[...] = src[...].astype(BF16)


def _cast_bf16(*weights):
    specs = [pl.BlockSpec((w.shape[0] // CAST_STEPS, w.shape[1]), lambda s: (s, 0))
             for w in weights]
    return pl.pallas_call(
        _cast_kernel,
        grid=(CAST_STEPS,),
        in_specs=specs,
        out_specs=specs,
        out_shape=[jax.ShapeDtypeStruct(w.shape, BF16) for w in weights],
        compiler_params=pltpu.CompilerParams(
            dimension_semantics=("parallel",), vmem_limit_bytes=VMEM_LIMIT),
        name="cast_bf16",
    )(*weights)


def _ffn1_mix_kernel(x_ref, posr_ref, g1_ref, wg_ref, wu_ref, wd_ref, gmix_ref, win_ref,
                     gq_ref, wuqt_ref, gkv_ref, wukt_ref, wvt_ref, gqkq_ref, gqkk_ref,
                     invc_ref, eye_ref,
                     h_ref, u_ref, qt_ref, k_ref, vt_ref, z_ref, kt_ref):
    half = MLA_ROPE // 2

    @pl.when(pl.program_id(0) == 0)
    def _no_previous_tile():
        z_ref[...] = jnp.zeros_like(z_ref)

    def norm_rope_heads(src, shared_rope, gain_t, cos_t, sin_t, dst_ref, lo, hi):
        for hd in range(lo, hi):
            t = src[hd * HEAD_PAD:(hd + 1) * HEAD_PAD, :]
            if shared_rope is not None:
                t = t + shared_rope
            ss = jnp.sum(t * t, axis=0, keepdims=True)
            t = t * lax.rsqrt(ss * (1.0 / MLA_QK) + EPS) * gain_t
            x1 = t[MLA_NOPE:MLA_NOPE + half, :]
            x2 = t[MLA_NOPE + half:MLA_QK, :]
            r0 = hd * HEAD_PAD
            dst_ref[r0:r0 + MLA_NOPE, :] = t[:MLA_NOPE, :].astype(BF16)
            dst_ref[r0 + MLA_NOPE:r0 + MLA_NOPE + half, :] = (x1 * cos_t - x2 * sin_t).astype(BF16)
            dst_ref[r0 + MLA_NOPE + half:r0 + MLA_QK, :] = (x1 * sin_t + x2 * cos_t).astype(BF16)
            dst_ref[r0 + MLA_QK:r0 + HEAD_PAD, :] = t[MLA_QK:, :].astype(BF16)

    carried = {}

    def q_part(lo, hi):
        norm_rope_heads(carried["qt"], None, carried["gq_t"], carried["cos_t"],
                        carried["sin_t"], qt_ref, lo, hi)

    def k_part(lo, hi):
        norm_rope_heads(carried["kt"], carried["kpe_t"], carried["gk_t"], carried["cos_t"],
                        carried["sin_t"], kt_ref, lo, hi)

    def mix_part0():
        c_q = _rms(z_ref[:, :Q_LORA], gq_ref[...]).astype(BF16)
        c_kv = _rms(z_ref[:, Q_LORA:Q_LORA + KV_LORA], gkv_ref[...]).astype(BF16)
        vt = _dot_nt(wvt_ref[...], c_kv)
        for c in range(ROW_TILE // ATT_TK):
            vt_ref[c] = vt[:, c * ATT_TK:(c + 1) * ATT_TK].astype(BF16)
        carried["qt"] = _dot_nt(wuqt_ref[...], c_q)
        carried["kt"] = _dot_nt(wukt_ref[...], c_kv)
        carried["kpe_t"] = z_ref[:, Q_LORA + KV_LORA:].T
        ang_t = invc_ref[...] * posr_ref[...]
        carried["cos_t"] = jnp.cos(ang_t)
        carried["sin_t"] = jnp.sin(ang_t)
        carried["gq_t"] = jnp.broadcast_to(
            gqkq_ref[...] * (MLA_QK ** -0.5 * math.log2(math.e)), (HEAD_PAD, ROW_TILE))
        carried["gk_t"] = jnp.broadcast_to(gqkk_ref[...], (HEAD_PAD, ROW_TILE))
        q_part(0, MLA_HEADS // 2)

    def mix_part1():
        q_part(MLA_HEADS // 2, MLA_HEADS)
        k_part(0, MLA_HEADS // 2)

    def mix_part2():
        k_part(MLA_HEADS // 2, MLA_HEADS)
        k_ref[...] = _dot_nt(eye_ref[...], kt_ref[...]).astype(BF16)

    halves = _row_groups()
    ns = [_rms(x_ref[rows, :], g1_ref[...]).astype(BF16) for rows in halves]
    accs = [None] * ROW_GROUPS
    for (c0, cw), mix_part in zip(FF_CHUNKS, (mix_part0, mix_part1, mix_part2)):
        for r in range(ROW_GROUPS):
            d = _swiglu_chunk(ns[r], wg_ref, wu_ref, wd_ref, c0, cw)
            accs[r] = d if accs[r] is None else accs[r] + d
        mix_part()
    for r, rows in enumerate(halves):
        h = x_ref[rows, :] + 0.5 * accs[r]
        h_ref[rows, :] = h
        z = _dot(_rms(h, gmix_ref[...]).astype(BF16), win_ref[...])
        u_ref[rows, :] = z[:, :SSM_WIDTH]
        z_ref[rows, :] = z[:, SSM_WIDTH:]


def _ffn1_mix(x, posr, *consts):
    tm = ROW_TILE
    n_tiles = BATCH * ROW_TILES

    def cur(s):
        t = jnp.minimum(s, n_tiles - 1)
        return t // ROW_TILES, t % ROW_TILES

    def prev(s):
        t = jnp.maximum(s - 1, 0)
        return t // ROW_TILES, t % ROW_TILES

    def row(w, which):
        return pl.BlockSpec((None, tm, w), lambda s: (*which(s), 0))

    col_prev = lambda rows: pl.BlockSpec((None, rows, tm), lambda s: (prev(s)[0], 0, prev(s)[1]))
    return pl.pallas_call(
        _ffn1_mix_kernel,
        grid=(n_tiles + 1,),
        in_specs=[row(D_MODEL, cur), col_prev(1)] + [_const_spec(c.shape) for c in consts],
        out_specs=[row(D_MODEL, cur), row(SSM_WIDTH, cur), col_prev(QK_PAD), row(QK_PAD, prev),
                   pl.BlockSpec((None, tm // ATT_TK, MLA_HEADS * MLA_V, ATT_TK),
                                lambda s: (*prev(s), 0, 0))],
        out_shape=[jax.ShapeDtypeStruct((BATCH, SEQ, D_MODEL), F32),
                   jax.ShapeDtypeStruct((BATCH, SEQ, SSM_WIDTH), F32),
                   jax.ShapeDtypeStruct((BATCH, QK_PAD, SEQ), BF16),
                   jax.ShapeDtypeStruct((BATCH, SEQ, QK_PAD), BF16),
                   jax.ShapeDtypeStruct((BATCH, SEQ // ATT_TK, MLA_HEADS * MLA_V, ATT_TK),
                                        BF16)],
        scratch_shapes=[pltpu.VMEM((tm, MIX_PAD), F32),
                        pltpu.VMEM((QK_PAD, tm), BF16)],
        compiler_params=pltpu.CompilerParams(
            dimension_semantics=("arbitrary",), vmem_limit_bytes=VMEM_LIMIT),
        name="ffn1_mix",
    )(x, posr, *consts)


def _s5_kernel(u_ref, unext_ref, perm_ref, permt_ref, lam_re_ref, lam_im_ref, logdt_ref,
               bre_ref, bim_ref, wcr_ref, wci_ref, d_ref, wglu_ref, bglu_ref,
               y_ref, wb_ref, lbar_ref, state_ref, bua_ref, bub_ref, sa_ref, sb_ref):
    nh = SSM_HALF_STATES

    def b_project(u_bt, bu_ref):
        u_tb = _dot(perm_ref[...], u_bt.astype(BF16)).astype(BF16)
        half_in = SSM_WIDTH // 2
        for k in range(2):
            bu_ref[:, k * 2 * nh:(k + 1) * 2 * nh] = _dot(u_tb[:, k * half_in:(k + 1) * half_in],
                                                          wb_ref[k])

    def chunk(ref, c):
        return ref[:, c * SCAN_T:(c + 1) * SCAN_T, :].reshape(SCAN_ROWS, SSM_WIDTH)

    @pl.when(pl.program_id(0) == 0)
    def _prepare():
        lr = lam_re_ref[...]
        li = lam_im_ref[...]
        dt = jnp.exp(logdt_ref[...])
        mag = jnp.exp(lr * dt)
        br = mag * jnp.cos(li * dt)
        bi = mag * jnp.sin(li * dt)
        den = lr * lr + li * li
        cr = ((br - 1.0) * lr + bi * li) / den
        ci = (bi * lr - (br - 1.0) * li) / den
        for k in range(2):
            ks = slice(k * nh, (k + 1) * nh)
            lbar_ref[k, 0:1, :] = br[:, ks]
            lbar_ref[k, 1:2, :] = bi[:, ks]
            wb_ref[k, :, :nh] = (bre_ref[k] * cr[:, ks] - bim_ref[k] * ci[:, ks]).astype(BF16)
            wb_ref[k, :, nh:] = (bre_ref[k] * ci[:, ks] + bim_ref[k] * cr[:, ks]).astype(BF16)
        state_ref[...] = jnp.zeros_like(state_ref)
        b_project(chunk(u_ref, 0), bua_ref)

    def scan(bu_ref, s_ref):
        for k in range(2):
            c_re = k * 2 * nh
            c_im = c_re + nh
            lr = jnp.broadcast_to(lbar_ref[k, 0:1, :], (SUBLANES, nh))
            li = jnp.broadcast_to(lbar_ref[k, 1:2, :], (SUBLANES, nh))
            sr = state_ref[k, 0]
            si = state_ref[k, 1]
            for t2 in range(SCAN_T // 2):
                packed_r, packed_i = [], []
                for r in (t2 * 2 * SUBLANES, (t2 * 2 + 1) * SUBLANES):
                    nr = lr * sr - li * si + bu_ref[r:r + SUBLANES, c_re:c_re + nh]
                    ni = lr * si + li * sr + bu_ref[r:r + SUBLANES, c_im:c_im + nh]
                    sr, si = nr, ni
                    packed_r.append(sr)
                    packed_i.append(si)
                rows = slice(t2 * 2 * SUBLANES, (t2 + 1) * 2 * SUBLANES)
                s_ref[rows, c_re:c_re + nh] = jnp.concatenate(packed_r, axis=0).astype(BF16)
                s_ref[rows, c_im:c_im + nh] = jnp.concatenate(packed_i, axis=0).astype(BF16)
            state_ref[k, 0] = sr
            state_ref[k, 1] = si

    def c_project(s_ref, u_bt, c):
        ys = []
        for k in range(2):
            c_re = k * 2 * nh
            c_im = c_re + nh
            ys.append(_dot(s_ref[:, c_re:c_re + nh], wcr_ref[k])
                      - _dot(s_ref[:, c_im:c_im + nh], wci_ref[k]))
        y_tb = jnp.concatenate(ys, axis=1).astype(BF16)
        y = _dot(permt_ref[...], y_tb) + d_ref[...] * u_bt
        y = jax.nn.gelu(y)
        y = y * jax.nn.sigmoid(_dot(y.astype(BF16), wglu_ref[...]) + bglu_ref[...])
        y_ref[:, c * SCAN_T:(c + 1) * SCAN_T, :] = y.reshape(BATCH, SCAN_T, SSM_WIDTH)

    b_project(chunk(u_ref, 1), bub_ref)
    scan(bua_ref, sa_ref)
    b_project(chunk(unext_ref, 0), bua_ref)
    scan(bub_ref, sb_ref)
    c_project(sa_ref, chunk(u_ref, 0), 0)
    c_project(sb_ref, chunk(u_ref, 1), 1)


def _s5(u, *consts):
    nh = SSM_HALF_STATES
    n_chunks = SEQ // SCAN_T
    blk = pl.BlockSpec((BATCH, 2 * SCAN_T, SSM_WIDTH), lambda g: (0, g, 0))
    nxt = pl.BlockSpec((BATCH, SCAN_T, SSM_WIDTH),
                       lambda g: (0, jnp.minimum(2 * g + 2, n_chunks - 1), 0))
    return pl.pallas_call(
        _s5_kernel,
        grid=(n_chunks // 2,),
        in_specs=[blk, nxt] + [_const_spec(c.shape) for c in consts],
        out_specs=blk,
        out_shape=jax.ShapeDtypeStruct((BATCH, SEQ, SSM_WIDTH), F32),
        scratch_shapes=[
            pltpu.VMEM((2, SSM_WIDTH // 2, 2 * nh), BF16),
            pltpu.VMEM((2, 2, nh), F32),
            pltpu.VMEM((2, 2, SUBLANES, nh), F32),
            pltpu.VMEM((SCAN_ROWS, 4 * nh), F32),
            pltpu.VMEM((SCAN_ROWS, 4 * nh), F32),
            pltpu.VMEM((SCAN_ROWS, 4 * nh), BF16),
            pltpu.VMEM((SCAN_ROWS, 4 * nh), BF16),
        ],
        compiler_params=pltpu.CompilerParams(
            dimension_semantics=("arbitrary",), vmem_limit_bytes=VMEM_LIMIT),
        name="s5",
    )(u, u, *consts)


def _attn_kernel(qt_ref, k_ref, vt_ref, o_ref, m_ref, acc_ref, sta_ref, stb_ref):
    qi = pl.program_id(1)
    tq, tk = ATT_TQ, ATT_TK
    kpos = lax.broadcasted_iota(jnp.int32, (tk, tq), 0)
    qpos = lax.broadcasted_iota(jnp.int32, (tk, tq), 1)
    ones_rows = (lax.broadcasted_iota(jnp.int32, (ATT_EXTRA, tk), 0) == 0).astype(BF16)
    m_ref[...] = jnp.full(m_ref.shape, -jnp.inf, F32)
    acc_ref[...] = jnp.zeros(acc_ref.shape, F32)

    def scores(j, st_ref):
        r0 = pl.multiple_of(j * tk, tk)
        for hd in range(MLA_HEADS):
            st_ref[hd] = _dot(k_ref[pl.ds(r0, tk), hd * HEAD_PAD:(hd + 1) * HEAD_PAD],
                              qt_ref[hd * HEAD_PAD:(hd + 1) * HEAD_PAD, :])

    def consume(j, st_ref, visible=None):
        for hd in range(MLA_HEADS):
            st = st_ref[hd]
            if visible is not None:
                st = jnp.where(visible, st, -jnp.inf)
            m_old = m_ref[hd]
            m_new = jnp.maximum(m_old, jnp.max(st, axis=0, keepdims=True))
            alpha = jnp.exp2(m_old - m_new)
            pt = jnp.exp2(st - m_new).astype(BF16)
            vt = jnp.concatenate([vt_ref[j, hd * MLA_V:(hd + 1) * MLA_V, :], ones_rows], axis=0)
            acc_ref[hd] = alpha * acc_ref[hd] + _dot(vt, pt)
            m_ref[hd] = m_new

    def tile_pair(j):
        scores(j + 1, stb_ref)
        consume(j, sta_ref)
        scores(j + 2, sta_ref)
        consume(j + 1, stb_ref)

    def tile_quad(jj, carry):
        tile_pair(4 * jj)
        tile_pair(4 * jj + 2)
        return carry

    scores(0, sta_ref)
    lax.fori_loop(0, qi // 2, tile_quad, 0)

    @pl.when(qi % 2 == 1)
    def _odd_pair():
        tile_pair(2 * qi - 2)

    j = 2 * qi
    scores(j + 1, stb_ref)
    consume(j, sta_ref, kpos <= qpos)
    consume(j + 1, stb_ref, kpos + tk <= qpos)

    outs = [acc_ref[hd, :MLA_V, :] / acc_ref[hd, MLA_V:MLA_V + 1, :] for hd in range(MLA_HEADS)]
    per_slab = LANES * 2 // MLA_V
    for c in range(MLA_HEADS // per_slab):
        o_ref[:, c * per_slab * MLA_V:(c + 1) * per_slab * MLA_V] = jnp.concatenate(
            outs[c * per_slab:(c + 1) * per_slab], axis=0).T


def _attn(qt, k, vt):
    width = MLA_HEADS * MLA_V
    return pl.pallas_call(
        _attn_kernel,
        grid=(BATCH, SEQ // ATT_TQ),
        in_specs=[pl.BlockSpec((None, QK_PAD, ATT_TQ), lambda b, i: (b, 0, i)),
                  pl.BlockSpec((None, SEQ, QK_PAD), lambda b, i: (b, 0, 0)),
                  pl.BlockSpec((None, SEQ // ATT_TK, width, ATT_TK), lambda b, i: (b, 0, 0, 0))],
        out_specs=pl.BlockSpec((None, ATT_TQ, width), lambda b, i: (b, i, 0)),
        out_shape=jax.ShapeDtypeStruct((BATCH, SEQ, width), F32),
        scratch_shapes=[pltpu.VMEM((MLA_HEADS, 1, ATT_TQ), F32),
                        pltpu.VMEM((MLA_HEADS, MLA_V + ATT_EXTRA, ATT_TQ), F32),
                        pltpu.VMEM((MLA_HEADS, ATT_TK, ATT_TQ), F32),
                        pltpu.VMEM((MLA_HEADS, ATT_TK, ATT_TQ), F32)],
        compiler_params=pltpu.CompilerParams(
            dimension_semantics=("parallel", "arbitrary"), vmem_limit_bytes=VMEM_LIMIT),
        name="attn",
    )(qt, k, vt)


def _out_ffn2_kernel(h_ref, ys_ref, ya_ref, p_ref, gs_ref, ga_ref, wos_ref, woa_ref,
                     g2_ref, wg_ref, wu_ref, wd_ref, gple_ref, wgate_ref, wproj_ref, o_ref):
    halves = _row_groups()
    hs, ns, accs = [], [], [None] * ROW_GROUPS
    for rows in halves:
        h = (h_ref[rows, :]
             + _dot(_rms(ys_ref[rows, :], gs_ref[...]).astype(BF16), wos_ref[...])
             + _dot(_rms(ya_ref[rows, :], ga_ref[...]).astype(BF16), woa_ref[...]))
        hs.append(h)
        ns.append(_rms(h, g2_ref[...]).astype(BF16))
    for c0, cw in FF_CHUNKS:
        for r in range(ROW_GROUPS):
            d = _swiglu_chunk(ns[r], wg_ref, wu_ref, wd_ref, c0, cw)
            accs[r] = d if accs[r] is None else accs[r] + d
    for r, rows in enumerate(halves):
        h = hs[r] + 0.5 * accs[r]
        gate = jax.nn.sigmoid(_dot(_rms(h, gple_ref[...]).astype(BF16), wgate_ref[...]))
        o_ref[rows, :] = h + gate * _dot(p_ref[rows, :].astype(BF16), wproj_ref[...])


def _out_ffn2(h, ys, ya, p, *consts):
    tm = ROW_TILE
    row = lambda w: pl.BlockSpec((None, tm, w), lambda b, i: (b, i, 0))
    return pl.pallas_call(
        _out_ffn2_kernel,
        grid=(BATCH, ROW_TILES),
        in_specs=[row(D_MODEL), row(SSM_WIDTH), row(SSM_WIDTH), row(PLE_DIM)]
                 + [_const_spec(c.shape) for c in consts],
        out_specs=row(D_MODEL),
        out_shape=jax.ShapeDtypeStruct((BATCH, SEQ, D_MODEL), F32),
        compiler_params=pltpu.CompilerParams(
            dimension_semantics=("parallel", "parallel"), vmem_limit_bytes=VMEM_LIMIT),
        name="out_ffn2",
    )(h, ys, ya, p, *consts)


def _pad_heads(w, width):
    w = w.reshape(w.shape[0], MLA_HEADS, width)
    return jnp.pad(w, ((0, 0), (0, 0), (0, HEAD_PAD - width))).reshape(w.shape[0], QK_PAD)


def _block_diag(blocks):
    g, r, c = blocks.shape
    eye = jnp.eye(g, dtype=blocks.dtype)
    return (blocks[:, :, None, :] * eye[:, None, :, None]).reshape(g * r, g * c)


def _permutation():
    rows = np.arange(SCAN_ROWS)
    t, b = rows // BATCH, rows % BATCH
    perm = np.zeros((SCAN_ROWS, SCAN_ROWS), np.float32)
    perm[rows, b * SCAN_T + t] = 1.0
    return perm


def kernel(x, p, positions, norm_ffn1, ffn1_w_gate, ffn1_w_up, ffn1_w_down, norm_mix, w_in,
           ssm_lam_re, ssm_lam_im, ssm_log_dt, ssm_b_re, ssm_b_im, ssm_c_re, ssm_c_im, ssm_d,
           ssm_w_glu, ssm_b_glu, mla_q_norm, mla_w_uq, mla_kv_norm, mla_w_ukv, mla_qk_norm_q,
           mla_qk_norm_k, out_norm_ssm, out_norm_attn, w_out, norm_ffn2, ffn2_w_gate, ffn2_w_up,
           ffn2_w_down, norm_ple, ple_w_gate, ple_w_proj):
    h = x
    for i in range(norm_ffn1.shape[0]):
        vec = lambda a: a[i].reshape(1, -1).astype(F32)
        bf = lambda a: a.astype(BF16)

        ffn_w = _cast_bf16(ffn1_w_gate[i], ffn1_w_up[i], ffn1_w_down[i],
                           ffn2_w_gate[i], ffn2_w_up[i], ffn2_w_down[i])

        pos = positions.astype(F32)
        win = jnp.zeros((D_MODEL, IN_PAD), F32)
        win = win.at[:, :IN_WIDTH - MLA_ROPE].set(w_in[i][:, :IN_WIDTH - MLA_ROPE])
        kpe0 = IN_WIDTH - MLA_ROPE + MLA_NOPE
        win = win.at[:, kpe0:kpe0 + MLA_ROPE].set(w_in[i][:, IN_WIDTH - MLA_ROPE:])
        wuqt = _pad_heads(mla_w_uq[i], MLA_QK).T
        wukv = mla_w_ukv[i].reshape(KV_LORA, MLA_HEADS, MLA_NOPE + MLA_V)
        wukt = _pad_heads(wukv[..., :MLA_NOPE].reshape(KV_LORA, -1), MLA_NOPE).T
        wvt = wukv[..., MLA_NOPE:].reshape(KV_LORA, -1).T
        pad_gain = lambda g: jnp.pad(g[i].astype(F32), (0, HEAD_PAD - MLA_QK)).reshape(HEAD_PAD, 1)
        half = MLA_ROPE // 2
        inv = ROPE_THETA ** (-jnp.arange(half, dtype=F32) / half)

        h1, u, qt, k, vt = _ffn1_mix(
            h, pos[:, None, :],
            vec(norm_ffn1), *ffn_w[:3],
            vec(norm_mix), bf(win), vec(mla_q_norm), bf(wuqt), vec(mla_kv_norm), bf(wukt), bf(wvt),
            pad_gain(mla_qk_norm_q), pad_gain(mla_qk_norm_k), inv.reshape(half, 1),
            jnp.eye(ROW_TILE, dtype=BF16))

        hg = SSM_GROUPS // 2
        cols = lambda a: a[i].astype(F32).reshape(1, SSM_GROUPS * SSM_STATE)
        logdt = jnp.repeat(ssm_log_dt[i].astype(F32), SSM_STATE).reshape(1, -1)
        b_t = lambda b: jnp.stack([_block_diag(jnp.swapaxes(b[i][kk * hg:(kk + 1) * hg], 1, 2))
                                   for kk in range(2)]).astype(F32)
        c_t = lambda c: jnp.stack([_block_diag(jnp.swapaxes(c[i][kk * hg:(kk + 1) * hg], 1, 2))
                                   for kk in range(2)]).astype(BF16)
        perm = _permutation()
        y_ssm = _s5(u, jnp.asarray(perm, BF16), jnp.asarray(perm.T, BF16),
                    cols(ssm_lam_re), cols(ssm_lam_im), logdt, b_t(ssm_b_re), b_t(ssm_b_im),
                    c_t(ssm_c_re), c_t(ssm_c_im), vec(ssm_d), bf(ssm_w_glu[i]), vec(ssm_b_glu))

        y_att = _attn(qt, k, vt)

        h = _out_ffn2(h1, y_ssm, y_att, p[i], vec(out_norm_ssm), vec(out_norm_attn),
                      bf(w_out[i][:SSM_WIDTH]), bf(w_out[i][SSM_WIDTH:]), vec(norm_ffn2),
                      *ffn_w[3:], vec(norm_ple),
                      bf(ple_w_gate[i]), bf(ple_w_proj[i]))
    return h.astype(x.dtype)
```

```python
import math

import jax
import jax.numpy as jnp
import numpy as np
from jax import lax
from jax.experimental import pallas as pl
from jax.experimental.pallas import tpu as pltpu

F32 = jnp.float32
BF16 = jnp.bfloat16

D_MODEL = 1024
BATCH = 8
SEQ = 4096
PLE_DIM = 256
D_FF = 2816
EPS = 1e-6
SSM_WIDTH = 512
SSM_GROUP = 16
SSM_GROUPS = 32
SSM_STATE = 64
MLA_HEADS = 8
MLA_NOPE = 64
MLA_ROPE = 32
MLA_QK = 96
MLA_V = 64
Q_LORA = 384
KV_LORA = 256
ROPE_THETA = 10000.0
IN_WIDTH = SSM_WIDTH + Q_LORA + KV_LORA + MLA_ROPE

LANES = 128
SUBLANES = 8
HEAD_PAD = LANES
QK_PAD = MLA_HEADS * HEAD_PAD
IN_PAD = 1280
MIX_PAD = IN_PAD - SSM_WIDTH
ROW_TILE = 512
ROW_TILES = SEQ // ROW_TILE
ROW_GROUPS = 2
FF_CHUNKS = ((0, 1024), (1024, 1024), (2048, 768))
SSM_HALF_STATES = SSM_GROUPS * SSM_STATE // 2
SCAN_T = 32
SCAN_ROWS = SCAN_T * BATCH
ATT_TQ = 512
ATT_TK = 256
ATT_EXTRA = 16
CAST_STEPS = 8
VMEM_LIMIT = 56 * 1024 * 1024
NT_DIMS = (((1,), (1,)), ((), ()))


def _const_spec(shape):
    nd = len(shape)
    return pl.BlockSpec(shape, lambda *_: (0,) * nd, pipeline_mode=pl.Buffered(1))


def _rms(x, g):
    ms = jnp.mean(x * x, axis=-1, keepdims=True)
    return x * lax.rsqrt(ms + EPS) * g


def _dot(a, b):
    return jnp.dot(a, b, preferred_element_type=F32)


def _dot_nt(a, b):
    return lax.dot_general(a, b, NT_DIMS, preferred_element_type=F32)


def _row_groups():
    rows = ROW_TILE // ROW_GROUPS
    return [slice(r * rows, (r + 1) * rows) for r in range(ROW_GROUPS)]


def _swiglu_chunk(n, wg_ref, wu_ref, wd_ref, c0, cw):
    g = _dot(n, wg_ref[:, c0:c0 + cw])
    u = _dot(n, wu_ref[:, c0:c0 + cw])
    a = (g * jax.nn.sigmoid(g) * u).astype(BF16)
    return _dot(a, wd_ref[c0:c0 + cw, :])


def _cast_kernel(*refs):
    n = len(refs) // 2
    for src, dst in zip(refs[:n], refs[n:]):
        dst[...] = src[...].astype(BF16)


def _cast_bf16(*weights):
    specs = [pl.BlockSpec((w.shape[0] // CAST_STEPS, w.shape[1]), lambda s: (s, 0))
             for w in weights]
    return pl.pallas_call(
        _cast_kernel,
        grid=(CAST_STEPS,),
        in_specs=specs,
        out_specs=specs,
        out_shape=[jax.ShapeDtypeStruct(w.shape, BF16) for w in weights],
        compiler_params=pltpu.CompilerParams(
            dimension_semantics=("parallel",), vmem_limit_bytes=VMEM_LIMIT),
        name="cast_bf16",
    )(*weights)


def _ffn1_mix_kernel(x_ref, posr_ref, g1_ref, wg_ref, wu_ref, wd_ref, gmix_ref, win_ref,
                     gq_ref, wuqt_ref, gkv_ref, wukt_ref, wvt_ref, gqkq_ref, gqkk_ref,
                     invc_ref, eye_ref,
                     h_ref, u_ref, qt_ref, k_ref, vt_ref, z_ref, kt_ref):
    half = MLA_ROPE // 2

    @pl.when(pl.program_id(0) == 0)
    def _no_previous_tile():
        z_ref[...] = jnp.zeros_like(z_ref)

    def norm_rope_heads(src, shared_rope, gain_t, cos_t, sin_t, dst_ref, lo, hi):
        for hd in range(lo, hi):
            r0 = hd * HEAD_PAD
            t = src[r0:r0 + MLA_QK, :]
            if shared_rope is not None:
                t = t + shared_rope
            ss = jnp.sum(t * t, axis=0, keepdims=True)
            t = t * lax.rsqrt(ss * (1.0 / MLA_QK) + EPS) * gain_t
            x1 = t[MLA_NOPE:MLA_NOPE + half, :]
            x2 = t[MLA_NOPE + half:MLA_QK, :]
            dst_ref[r0:r0 + MLA_NOPE, :] = t[:MLA_NOPE, :].astype(BF16)
            dst_ref[r0 + MLA_NOPE:r0 + MLA_NOPE + half, :] = (x1 * cos_t - x2 * sin_t).astype(BF16)
            dst_ref[r0 + MLA_NOPE + half:r0 + MLA_QK, :] = (x1 * sin_t + x2 * cos_t).astype(BF16)
            dst_ref[r0 + MLA_QK:r0 + HEAD_PAD, :] = jnp.zeros((HEAD_PAD - MLA_QK, ROW_TILE), BF16)

    carried = {}

    def q_part(lo, hi):
        norm_rope_heads(carried["qt"], None, carried["gq_t"], carried["cos_t"],
                        carried["sin_t"], qt_ref, lo, hi)

    def k_part(lo, hi):
        norm_rope_heads(carried["kt"], carried["kpe_t"], carried["gk_t"], carried["cos_t"],
                        carried["sin_t"], kt_ref, lo, hi)

    def mix_part0():
        c_q = _rms(z_ref[:, :Q_LORA], gq_ref[...]).astype(BF16)
        c_kv = _rms(z_ref[:, Q_LORA:Q_LORA + KV_LORA], gkv_ref[...]).astype(BF16)
        vt = _dot_nt(wvt_ref[...], c_kv)
        for c in range(ROW_TILE // ATT_TK):
            vt_ref[c] = vt[:, c * ATT_TK:(c + 1) * ATT_TK].astype(BF16)
        carried["qt"] = _dot_nt(wuqt_ref[...], c_q)
        carried["kt"] = _dot_nt(wukt_ref[...], c_kv)
        carried["kpe_t"] = z_ref[:, Q_LORA + KV_LORA:].T[:MLA_QK, :]
        ang_t = invc_ref[...] * posr_ref[...]
        carried["cos_t"] = jnp.cos(ang_t)
        carried["sin_t"] = jnp.sin(ang_t)
        carried["gq_t"] = jnp.broadcast_to(
            gqkq_ref[:MLA_QK, :] * (MLA_QK ** -0.5 * math.log2(math.e)), (MLA_QK, ROW_TILE))
        carried["gk_t"] = jnp.broadcast_to(gqkk_ref[:MLA_QK, :], (MLA_QK, ROW_TILE))
        q_part(0, MLA_HEADS // 2)

    def mix_part1():
        q_part(MLA_HEADS // 2, MLA_HEADS)
        k_part(0, MLA_HEADS // 2)

    def mix_part2():
        k_part(MLA_HEADS // 2, MLA_HEADS)
        k_ref[...] = _dot_nt(eye_ref[...], kt_ref[...]).astype(BF16)

    n_tiles = BATCH * ROW_TILES

    @pl.when(pl.program_id(0) < n_tiles)
    def _tile_and_previous_mix():
        halves = _row_groups()
        ns = [_rms(x_ref[rows, :], g1_ref[...]).astype(BF16) for rows in halves]
        accs = [None] * ROW_GROUPS
        for (c0, cw), mix_part in zip(FF_CHUNKS, (mix_part0, mix_part1, mix_part2)):
            for r in range(ROW_GROUPS):
                d = _swiglu_chunk(ns[r], wg_ref, wu_ref, wd_ref, c0, cw)
                accs[r] = d if accs[r] is None else accs[r] + d
            mix_part()
        for r, rows in enumerate(halves):
            h = x_ref[rows, :] + 0.5 * accs[r]
            h_ref[rows, :] = h
            z = _dot(_rms(h, gmix_ref[...]).astype(BF16), win_ref[...])
            u_ref[rows, :] = z[:, :SSM_WIDTH]
            z_ref[rows, :] = z[:, SSM_WIDTH:]

    @pl.when(pl.program_id(0) == n_tiles)
    def _last_mix_only():
        mix_part0()
        mix_part1()
        mix_part2()


def _ffn1_mix(x, posr, *consts):
    tm = ROW_TILE
    n_tiles = BATCH * ROW_TILES

    def cur(s):
        t = jnp.minimum(s, n_tiles - 1)
        return t // ROW_TILES, t % ROW_TILES

    def prev(s):
        t = jnp.maximum(s - 1, 0)
        return t // ROW_TILES, t % ROW_TILES

    def row(w, which):
        return pl.BlockSpec((None, tm, w), lambda s: (*which(s), 0))

    col_prev = lambda rows: pl.BlockSpec((None, rows, tm), lambda s: (prev(s)[0], 0, prev(s)[1]))
    return pl.pallas_call(
        _ffn1_mix_kernel,
        grid=(n_tiles + 1,),
        in_specs=[row(D_MODEL, cur), col_prev(1)] + [_const_spec(c.shape) for c in consts],
        out_specs=[row(D_MODEL, cur), row(SSM_WIDTH, cur), col_prev(QK_PAD), row(QK_PAD, prev),
                   pl.BlockSpec((None, tm // ATT_TK, MLA_HEADS * MLA_V, ATT_TK),
                                lambda s: (*prev(s), 0, 0))],
        out_shape=[jax.ShapeDtypeStruct((BATCH, SEQ, D_MODEL), F32),
                   jax.ShapeDtypeStruct((BATCH, SEQ, SSM_WIDTH), F32),
                   jax.ShapeDtypeStruct((BATCH, QK_PAD, SEQ), BF16),
                   jax.ShapeDtypeStruct((BATCH, SEQ, QK_PAD), BF16),
                   jax.ShapeDtypeStruct((BATCH, SEQ // ATT_TK, MLA_HEADS * MLA_V, ATT_TK),
                                        BF16)],
        scratch_shapes=[pltpu.VMEM((tm, MIX_PAD), F32),
                        pltpu.VMEM((QK_PAD, tm), BF16)],
        compiler_params=pltpu.CompilerParams(
            dimension_semantics=("arbitrary",), vmem_limit_bytes=VMEM_LIMIT),
        name="ffn1_mix",
    )(x, posr, *consts)


def _s5_kernel(u_ref, unext_ref, perm_ref, permt_ref, lam_re_ref, lam_im_ref, logdt_ref,
               bre_ref, bim_ref, wcr_ref, wci_ref, d_ref, wglu_ref, bglu_ref,
               y_ref, wb_ref, lbar_ref, state_ref, bua_ref, bub_ref, sa_ref, sb_ref):
    nh = SSM_HALF_STATES

    def b_project(u_bt, bu_ref):
        u_tb = _dot(perm_ref[...], u_bt.astype(BF16)).astype(BF16)
        half_in = SSM_WIDTH // 2
        for k in range(2):
            bu_ref[:, k * 2 * nh:(k + 1) * 2 * nh] = _dot(u_tb[:, k * half_in:(k + 1) * half_in],
                                                          wb_ref[k])

    def chunk(ref, c):
        return ref[:, c * SCAN_T:(c + 1) * SCAN_T, :].reshape(SCAN_ROWS, SSM_WIDTH)

    @pl.when(pl.program_id(0) == 0)
    def _prepare():
        lr = lam_re_ref[...]
        li = lam_im_ref[...]
        dt = jnp.exp(logdt_ref[...])
        mag = jnp.exp(lr * dt)
        br = mag * jnp.cos(li * dt)
        bi = mag * jnp.sin(li * dt)
        den = lr * lr + li * li
        cr = ((br - 1.0) * lr + bi * li) / den
        ci = (bi * lr - (br - 1.0) * li) / den
        for k in range(2):
            ks = slice(k * nh, (k + 1) * nh)
            lbar_ref[k, 0:1, :] = br[:, ks]
            lbar_ref[k, 1:2, :] = bi[:, ks]
            wb_ref[k, :, :nh] = (bre_ref[k] * cr[:, ks] - bim_ref[k] * ci[:, ks]).astype(BF16)
            wb_ref[k, :, nh:] = (bre_ref[k] * ci[:, ks] + bim_ref[k] * cr[:, ks]).astype(BF16)
        state_ref[...] = jnp.zeros_like(state_ref)
        b_project(chunk(u_ref, 0), bua_ref)

    def scan(bu_ref, s_ref):
        for k in range(2):
            c_re = k * 2 * nh
            c_im = c_re + nh
            lr = jnp.broadcast_to(lbar_ref[k, 0:1, :], (SUBLANES, nh))
            li = jnp.broadcast_to(lbar_ref[k, 1:2, :], (SUBLANES, nh))
            sr = state_ref[k, 0]
            si = state_ref[k, 1]
            for t2 in range(SCAN_T // 2):
                packed_r, packed_i = [], []
                for r in (t2 * 2 * SUBLANES, (t2 * 2 + 1) * SUBLANES):
                    nr = lr * sr - li * si + bu_ref[r:r + SUBLANES, c_re:c_re + nh]
                    ni = lr * si + li * sr + bu_ref[r:r + SUBLANES, c_im:c_im + nh]
                    sr, si = nr, ni
                    packed_r.append(sr)
                    packed_i.append(si)
                rows = slice(t2 * 2 * SUBLANES, (t2 + 1) * 2 * SUBLANES)
                s_ref[rows, c_re:c_re + nh] = jnp.concatenate(packed_r, axis=0).astype(BF16)
                s_ref[rows, c_im:c_im + nh] = jnp.concatenate(packed_i, axis=0).astype(BF16)
            state_ref[k, 0] = sr
            state_ref[k, 1] = si

    def c_project(s_ref, u_bt, c):
        ys = []
        for k in range(2):
            c_re = k * 2 * nh
            c_im = c_re + nh
            ys.append(_dot(s_ref[:, c_re:c_re + nh], wcr_ref[k])
                      - _dot(s_ref[:, c_im:c_im + nh], wci_ref[k]))
        y_tb = jnp.concatenate(ys, axis=1).astype(BF16)
        y = _dot(permt_ref[...], y_tb) + d_ref[...] * u_bt
        y = jax.nn.gelu(y)
        y = y * jax.nn.sigmoid(_dot(y.astype(BF16), wglu_ref[...]) + bglu_ref[...])
        y_ref[:, c * SCAN_T:(c + 1) * SCAN_T, :] = y.reshape(BATCH, SCAN_T, SSM_WIDTH)

    b_project(chunk(u_ref, 1), bub_ref)
    scan(bua_ref, sa_ref)
    b_project(chunk(unext_ref, 0), bua_ref)
    scan(bub_ref, sb_ref)
    c_project(sa_ref, chunk(u_ref, 0), 0)
    c_project(sb_ref, chunk(u_ref, 1), 1)


def _s5(u, *consts):
    nh = SSM_HALF_STATES
    n_chunks = SEQ // SCAN_T
    blk = pl.BlockSpec((BATCH, 2 * SCAN_T, SSM_WIDTH), lambda g: (0, g, 0))
    nxt = pl.BlockSpec((BATCH, SCAN_T, SSM_WIDTH),
                       lambda g: (0, jnp.minimum(2 * g + 2, n_chunks - 1), 0))
    return pl.pallas_call(
        _s5_kernel,
        grid=(n_chunks // 2,),
        in_specs=[blk, nxt] + [_const_spec(c.shape) for c in consts],
        out_specs=blk,
        out_shape=jax.ShapeDtypeStruct((BATCH, SEQ, SSM_WIDTH), F32),
        scratch_shapes=[
            pltpu.VMEM((2, SSM_WIDTH // 2, 2 * nh), BF16),
            pltpu.VMEM((2, 2, nh), F32),
            pltpu.VMEM((2, 2, SUBLANES, nh), F32),
            pltpu.VMEM((SCAN_ROWS, 4 * nh), F32),
            pltpu.VMEM((SCAN_ROWS, 4 * nh), F32),
            pltpu.VMEM((SCAN_ROWS, 4 * nh), BF16),
            pltpu.VMEM((SCAN_ROWS, 4 * nh), BF16),
        ],
        compiler_params=pltpu.CompilerParams(
            dimension_semantics=("arbitrary",), vmem_limit_bytes=VMEM_LIMIT),
        name="s5",
    )(u, u, *consts)


def _attn_kernel(qt_ref, k_ref, vt_ref, o_ref, m_ref, acc_ref, sta_ref, stb_ref):
    qi = pl.program_id(1)
    tq, tk = ATT_TQ, ATT_TK
    kpos = lax.broadcasted_iota(jnp.int32, (tk, tq), 0)
    qpos = lax.broadcasted_iota(jnp.int32, (tk, tq), 1)
    ones_rows = (lax.broadcasted_iota(jnp.int32, (ATT_EXTRA, tk), 0) == 0).astype(BF16)
    m_ref[...] = jnp.full(m_ref.shape, -jnp.inf, F32)
    acc_ref[...] = jnp.zeros(acc_ref.shape, F32)

    def scores(j, st_ref):
        r0 = pl.multiple_of(j * tk, tk)
        for hd in range(MLA_HEADS):
            st_ref[hd] = _dot(k_ref[pl.ds(r0, tk), hd * HEAD_PAD:(hd + 1) * HEAD_PAD],
                              qt_ref[hd * HEAD_PAD:(hd + 1) * HEAD_PAD, :])

    def consume(j, st_ref, visible=None):
        for hd in range(MLA_HEADS):
            st = st_ref[hd]
            if visible is not None:
                st = jnp.where(visible, st, -jnp.inf)
            m_old = m_ref[hd]
            m_new = jnp.maximum(m_old, jnp.max(st, axis=0, keepdims=True))
            alpha = jnp.exp2(m_old - m_new)
            pt = jnp.exp2(st - m_new).astype(BF16)
            vt = jnp.concatenate([vt_ref[j, hd * MLA_V:(hd + 1) * MLA_V, :], ones_rows], axis=0)
            acc_ref[hd] = alpha * acc_ref[hd] + _dot(vt, pt)
            m_ref[hd] = m_new

    def tile_pair(j):
        scores(j + 1, stb_ref)
        consume(j, sta_ref)
        scores(j + 2, sta_ref)
        consume(j + 1, stb_ref)

    def tile_quad(jj, carry):
        tile_pair(4 * jj + 2)
        tile_pair(4 * jj + 4)
        return carry

    @pl.when(qi == 0)
    def _only_diagonal():
        scores(0, sta_ref)

    @pl.when(qi > 0)
    def _first_pair():
        scores(0, sta_ref)
        tile_pair(0)

    rest = jnp.maximum(qi - 1, 0)
    lax.fori_loop(0, rest // 2, tile_quad, 0)

    @pl.when(rest % 2 == 1)
    def _last_pair():
        tile_pair(2 * qi - 2)

    j = 2 * qi
    scores(j + 1, stb_ref)
    consume(j, sta_ref, kpos <= qpos)
    consume(j + 1, stb_ref, kpos + tk <= qpos)

    outs = [acc_ref[hd, :MLA_V, :] / acc_ref[hd, MLA_V:MLA_V + 1, :] for hd in range(MLA_HEADS)]
    per_slab = LANES * 2 // MLA_V
    for c in range(MLA_HEADS // per_slab):
        o_ref[:, c * per_slab * MLA_V:(c + 1) * per_slab * MLA_V] = jnp.concatenate(
            outs[c * per_slab:(c + 1) * per_slab], axis=0).T


def _attn(qt, k, vt):
    width = MLA_HEADS * MLA_V
    return pl.pallas_call(
        _attn_kernel,
        grid=(BATCH, SEQ // ATT_TQ),
        in_specs=[pl.BlockSpec((None, QK_PAD, ATT_TQ), lambda b, i: (b, 0, i)),
                  pl.BlockSpec((None, SEQ, QK_PAD), lambda b, i: (b, 0, 0)),
                  pl.BlockSpec((None, SEQ // ATT_TK, width, ATT_TK), lambda b, i: (b, 0, 0, 0))],
        out_specs=pl.BlockSpec((None, ATT_TQ, width), lambda b, i: (b, i, 0)),
        out_shape=jax.ShapeDtypeStruct((BATCH, SEQ, width), F32),
        scratch_shapes=[pltpu.VMEM((MLA_HEADS, 1, ATT_TQ), F32),
                        pltpu.VMEM((MLA_HEADS, MLA_V + ATT_EXTRA, ATT_TQ), F32),
                        pltpu.VMEM((MLA_HEADS, ATT_TK, ATT_TQ), F32),
                        pltpu.VMEM((MLA_HEADS, ATT_TK, ATT_TQ), F32)],
        compiler_params=pltpu.CompilerParams(
            dimension_semantics=("parallel", "arbitrary"), vmem_limit_bytes=VMEM_LIMIT),
        name="attn",
    )(qt, k, vt)


def _out_ffn2_kernel(h_ref, ys_ref, ya_ref, p_ref, gs_ref, ga_ref, wos_ref, woa_ref,
                     g2_ref, wg_ref, wu_ref, wd_ref, gple_ref, wgate_ref, wproj_ref, o_ref):
    halves = _row_groups()
    hs, ns, accs = [], [], [None] * ROW_GROUPS
    for rows in halves:
        h = (h_ref[rows, :]
             + _dot(_rms(ys_ref[rows, :], gs_ref[...]).astype(BF16), wos_ref[...])
             + _dot(_rms(ya_ref[rows, :], ga_ref[...]).astype(BF16), woa_ref[...]))
        hs.append(h)
        ns.append(_rms(h, g2_ref[...]).astype(BF16))
    for c0, cw in FF_CHUNKS:
        for r in range(ROW_GROUPS):
            d = _swiglu_chunk(ns[r], wg_ref, wu_ref, wd_ref, c0, cw)
            accs[r] = d if accs[r] is None else accs[r] + d
    for r, rows in enumerate(halves):
        h = hs[r] + 0.5 * accs[r]
        gate = jax.nn.sigmoid(_dot(_rms(h, gple_ref[...]).astype(BF16), wgate_ref[...]))
        o_ref[rows, :] = h + gate * _dot(p_ref[rows, :].astype(BF16), wproj_ref[...])


def _out_ffn2(h, ys, ya, p, *consts):
    tm = ROW_TILE
    row = lambda w: pl.BlockSpec((None, tm, w), lambda b, i: (b, i, 0))
    return pl.pallas_call(
        _out_ffn2_kernel,
        grid=(BATCH, ROW_TILES),
        in_specs=[row(D_MODEL), row(SSM_WIDTH), row(SSM_WIDTH), row(PLE_DIM)]
                 + [_const_spec(c.shape) for c in consts],
        out_specs=row(D_MODEL),
        out_shape=jax.ShapeDtypeStruct((BATCH, SEQ, D_MODEL), F32),
        compiler_params=pltpu.CompilerParams(
            dimension_semantics=("parallel", "parallel"), vmem_limit_bytes=VMEM_LIMIT),
        name="out_ffn2",
    )(h, ys, ya, p, *consts)


def _pad_heads(w, width):
    w = w.reshape(w.shape[0], MLA_HEADS, width)
    return jnp.pad(w, ((0, 0), (0, 0), (0, HEAD_PAD - width))).reshape(w.shape[0], QK_PAD)


def _block_diag(blocks):
    g, r, c = blocks.shape
    eye = jnp.eye(g, dtype=blocks.dtype)
    return (blocks[:, :, None, :] * eye[:, None, :, None]).reshape(g * r, g * c)


def _permutation():
    rows = np.arange(SCAN_ROWS)
    t, b = rows // BATCH, rows % BATCH
    perm = np.zeros((SCAN_ROWS, SCAN_ROWS), np.float32)
    perm[rows, b * SCAN_T + t] = 1.0
    return perm


def kernel(x, p, positions, norm_ffn1, ffn1_w_gate, ffn1_w_up, ffn1_w_down, norm_mix, w_in,
           ssm_lam_re, ssm_lam_im, ssm_log_dt, ssm_b_re, ssm_b_im, ssm_c_re, ssm_c_im, ssm_d,
           ssm_w_glu, ssm_b_glu, mla_q_norm, mla_w_uq, mla_kv_norm, mla_w_ukv, mla_qk_norm_q,
           mla_qk_norm_k, out_norm_ssm, out_norm_attn, w_out, norm_ffn2, ffn2_w_gate, ffn2_w_up,
           ffn2_w_down, norm_ple, ple_w_gate, ple_w_proj):
    h = x
    for i in range(norm_ffn1.shape[0]):
        vec = lambda a: a[i].reshape(1, -1).astype(F32)
        bf = lambda a: a.astype(BF16)

        ffn_w = _cast_bf16(ffn1_w_gate[i], ffn1_w_up[i], ffn1_w_down[i],
                           ffn2_w_gate[i], ffn2_w_up[i], ffn2_w_down[i])

        pos = positions.astype(F32)
        win = jnp.zeros((D_MODEL, IN_PAD), F32)
        win = win.at[:, :IN_WIDTH - MLA_ROPE].set(w_in[i][:, :IN_WIDTH - MLA_ROPE])
        kpe0 = IN_WIDTH - MLA_ROPE + MLA_NOPE
        win = win.at[:, kpe0:kpe0 + MLA_ROPE].set(w_in[i][:, IN_WIDTH - MLA_ROPE:])
        wuqt = _pad_heads(mla_w_uq[i], MLA_QK).T
        wukv = mla_w_ukv[i].reshape(KV_LORA, MLA_HEADS, MLA_NOPE + MLA_V)
        wukt = _pad_heads(wukv[..., :MLA_NOPE].reshape(KV_LORA, -1), MLA_NOPE).T
        wvt = wukv[..., MLA_NOPE:].reshape(KV_LORA, -1).T
        pad_gain = lambda g: jnp.pad(g[i].astype(F32), (0, HEAD_PAD - MLA_QK)).reshape(HEAD_PAD, 1)
        half = MLA_ROPE // 2
        inv = ROPE_THETA ** (-jnp.arange(half, dtype=F32) / half)

        h1, u, qt, k, vt = _ffn1_mix(
            h, pos[:, None, :],
            vec(norm_ffn1), *ffn_w[:3],
            vec(norm_mix), bf(win), vec(mla_q_norm), bf(wuqt), vec(mla_kv_norm), bf(wukt), bf(wvt),
            pad_gain(mla_qk_norm_q), pad_gain(mla_qk_norm_k), inv.reshape(half, 1),
            jnp.eye(ROW_TILE, dtype=BF16))

        hg = SSM_GROUPS // 2
        cols = lambda a: a[i].astype(F32).reshape(1, SSM_GROUPS * SSM_STATE)
        logdt = jnp.repeat(ssm_log_dt[i].astype(F32), SSM_STATE).reshape(1, -1)
        b_t = lambda b: jnp.stack([_block_diag(jnp.swapaxes(b[i][kk * hg:(kk + 1) * hg], 1, 2))
                                   for kk in range(2)]).astype(F32)
        c_t = lambda c: jnp.stack([_block_diag(jnp.swapaxes(c[i][kk * hg:(kk + 1) * hg], 1, 2))
                                   for kk in range(2)]).astype(BF16)
        perm = _permutation()
        y_ssm = _s5(u, jnp.asarray(perm, BF16), jnp.asarray(perm.T, BF16),
                    cols(ssm_lam_re), cols(ssm_lam_im), logdt, b_t(ssm_b_re), b_t(ssm_b_im),
                    c_t(ssm_c_re), c_t(ssm_c_im), vec(ssm_d), bf(ssm_w_glu[i]), vec(ssm_b_glu))

        y_att = _attn(qt, k, vt)

        h = _out_ffn2(h1, y_ssm, y_att, p[i], vec(out_norm_ssm), vec(out_norm_attn),
                      bf(w_out[i][:SSM_WIDTH]), bf(w_out[i][SSM_WIDTH:]), vec(norm_ffn2),
                      *ffn_w[3:], vec(norm_ple),
                      bf(ple_w_gate[i]), bf(ple_w_proj[i]))
    return h.astype(x.dtype)
```
